```python
import math
import jax, jax.numpy as jnp
from jax import lax
import numpy as np

D_MODEL = 1024
BATCH = 8
SEQ = 4096
DEPTH = 1

ATTN_HEADS = 8
KV_HEADS = 2
HEAD_DIM = 64
GQA_GROUP = ATTN_HEADS // KV_HEADS
ATTN_WIDTH = ATTN_HEADS * HEAD_DIM
KV_WIDTH = KV_HEADS * HEAD_DIM
WINDOW = 128
ATTN_BLOCK = 128
ROPE_DIM = HEAD_DIM // 4
ROPE_THETA = 500000.0
RNN_WIDTH = D_MODEL // 2
RNN_HEADS = 8
RNN_HEAD_DIM = RNN_WIDTH // RNN_HEADS
CONV_WIDTH = 4
LRU_C = 8.0
MIX_WIDTH = ATTN_WIDTH + RNN_WIDTH
IN_WIDTH = ATTN_WIDTH + 2 * KV_WIDTH + 2 * RNN_WIDTH
PEER_HEADS = 8
PEER_N_KEYS = 128
PEER_N_EXPERTS = PEER_N_KEYS * PEER_N_KEYS
PEER_HALF_DIM = 128
PEER_TOPK = 16
PEER_CHUNK = 128
ALPHA = (2.0 * DEPTH) ** 0.25
BETA = (8.0 * DEPTH) ** -0.25
EPS = 1e-5

kernel_name = "hymba_hawk_swa_sink_peer_deepnorm"


def layer_norm(x, g, b):
    xf = x.astype(jnp.float32)
    mu = jnp.mean(xf, axis=-1, keepdims=True)
    var = jnp.mean(jnp.square(xf - mu), axis=-1, keepdims=True)
    return ((xf - mu) * lax.rsqrt(var + EPS)).astype(x.dtype) * g + b


def rms_norm(x, g):
    xf = x.astype(jnp.float32)
    return (xf * lax.rsqrt(jnp.mean(jnp.square(xf), axis=-1, keepdims=True) + EPS)).astype(x.dtype) * g


def partial_rotary(t, cos, sin):
    half = ROPE_DIM // 2
    tf = t.astype(jnp.float32)
    t1, t2, rest = tf[..., :half], tf[..., half:ROPE_DIM], tf[..., ROPE_DIM:]
    out = jnp.concatenate([t1 * cos - t2 * sin, t2 * cos + t1 * sin, rest], axis=-1)
    return out.astype(t.dtype)


def sliding_window_attention(q, k, v, sinks):
    B, S = q.shape[0], q.shape[1]
    nb = S // ATTN_BLOCK
    qb = q.reshape(B, nb, ATTN_BLOCK, KV_HEADS, GQA_GROUP, HEAD_DIM)
    kb = k.reshape(B, nb, ATTN_BLOCK, KV_HEADS, HEAD_DIM)
    vb = v.reshape(B, nb, ATTN_BLOCK, KV_HEADS, HEAD_DIM)
    k_prev = jnp.concatenate([jnp.zeros_like(kb[:, :1]), kb[:, :-1]], axis=1)
    v_prev = jnp.concatenate([jnp.zeros_like(vb[:, :1]), vb[:, :-1]], axis=1)
    kw = jnp.concatenate([k_prev, kb], axis=2)
    vw = jnp.concatenate([v_prev, vb], axis=2)
    s = jnp.einsum('bnqhgd,bnkhd->bnhgqk', qb, kw).astype(jnp.float32) * (HEAD_DIM ** -0.5)
    qi = jnp.arange(ATTN_BLOCK)[:, None]
    ki = jnp.arange(2 * ATTN_BLOCK)[None, :]
    diff = ATTN_BLOCK + qi - ki
    blk = jnp.arange(nb)[:, None, None]
    kpos = (blk - 1) * ATTN_BLOCK + ki[None]
    valid = (diff >= 0)[None] & (diff < WINDOW)[None] & (kpos >= 0)
    s = jnp.where(valid[None, :, None, None], s, -jnp.inf)
    sink = sinks.astype(jnp.float32).reshape(KV_HEADS, GQA_GROUP)[None, None, :, :, None, None]
    m = jnp.maximum(jnp.max(s, axis=-1, keepdims=True), sink)
    p = jnp.exp(s - m)
    denom = jnp.sum(p, axis=-1, keepdims=True) + jnp.exp(sink - m)
    probs = (p / denom).astype(v.dtype)
    o = jnp.einsum('bnhgqk,bnkhd->bnqhgd', probs, vw)
    return o.reshape(B, S, ATTN_WIDTH)


def rg_lru_branch(xr, gate_in, conv_w, conv_b, gate_a_w, gate_a_b, gate_x_w, gate_x_b, lru_lambda):
    B, S, C = xr.shape
    xc = lax.conv_general_dilated(
        xr, conv_w.reshape(CONV_WIDTH, 1, C), window_strides=(1,),
        padding=[(CONV_WIDTH - 1, 0)], dimension_numbers=('NWC', 'WIO', 'NWC'),
        feature_group_count=C) + conv_b
    xh = xc.reshape(B, S, RNN_HEADS, RNN_HEAD_DIM)
    r = jax.nn.sigmoid(jnp.einsum('bshi,hij->bshj', xh, gate_a_w).reshape(B, S, C) + gate_a_b)
    i = jax.nn.sigmoid(jnp.einsum('bshi,hij->bshj', xh, gate_x_w).reshape(B, S, C) + gate_x_b)
    log_a = -LRU_C * r.astype(jnp.float32) * jax.nn.softplus(-lru_lambda.astype(jnp.float32))
    a = jnp.exp(log_a)
    b = jnp.sqrt(-jnp.expm1(2.0 * log_a)) * (i * xc).astype(jnp.float32)

    def combine(left, right):
        a_l, b_l = left
        a_r, b_r = right
        return a_l * a_r, a_r * b_l + b_r

    _, h = lax.associative_scan(combine, (a, b), axis=1)
    return jax.nn.gelu(gate_in) * h.astype(xr.dtype)


def peer_ffn(h, peer_w_q, peer_keys_1, peer_keys_2, peer_u, peer_v):
    B, S, D = h.shape
    T = B * S
    ht = h.reshape(T, D)
    q = (ht @ peer_w_q).reshape(T, PEER_HEADS, 2, PEER_HALF_DIM)
    s1 = jnp.einsum('thd,kd->thk', q[:, :, 0], peer_keys_1)
    s2 = jnp.einsum('thd,kd->thk', q[:, :, 1], peer_keys_2)
    v1, i1 = lax.top_k(s1, PEER_TOPK)
    v2, i2 = lax.top_k(s2, PEER_TOPK)
    cand = (v1[..., :, None] + v2[..., None, :]).reshape(T, PEER_HEADS, PEER_TOPK * PEER_TOPK)
    sc, ci = lax.top_k(cand, PEER_TOPK)
    e1 = jnp.take_along_axis(i1, ci // PEER_TOPK, axis=-1)
    e2 = jnp.take_along_axis(i2, ci % PEER_TOPK, axis=-1)
    experts = (e1 * PEER_N_KEYS + e2).reshape(T, PEER_HEADS * PEER_TOPK)
    gates = jax.nn.softmax(sc.astype(jnp.float32), axis=-1).astype(h.dtype).reshape(T, PEER_HEADS * PEER_TOPK)
    n_chunks = T // PEER_CHUNK

    def chunk_fn(args):
        hc, ec, gc = args
        u = peer_u[ec]
        act = jax.nn.gelu(jnp.einsum('cd,ced->ce', hc, u))
        return jnp.einsum('ce,ced->cd', gc * act, peer_v[ec])

    out = lax.map(chunk_fn, (ht.reshape(n_chunks, PEER_CHUNK, D),
                             experts.reshape(n_chunks, PEER_CHUNK, -1),
                             gates.reshape(n_chunks, PEER_CHUNK, -1)))
    return out.reshape(B, S, D)


def setup_inputs(seed: int = 0) -> dict:
    key = jax.random.key(seed)
    ks = jax.random.split(key, 24)
    f32 = jnp.float32
    nrm = lambda k, shape, s: jax.random.normal(k, shape, f32) * s
    x = jax.random.normal(ks[0], (BATCH, SEQ, D_MODEL), f32)
    w_in = nrm(ks[1], (D_MODEL, IN_WIDTH), D_MODEL ** -0.5)
    v_lo = ATTN_WIDTH + KV_WIDTH
    w_in = w_in.at[:, v_lo:v_lo + KV_WIDTH].multiply(BETA)
    b_in = nrm(ks[2], (IN_WIDTH,), 0.01)
    attn_sinks = nrm(ks[3], (ATTN_HEADS,), 0.5)
    conv_w = nrm(ks[4], (CONV_WIDTH, RNN_WIDTH), CONV_WIDTH ** -0.5)
    conv_b = nrm(ks[5], (RNN_WIDTH,), 0.01)
    gate_a_w = nrm(ks[6], (RNN_HEADS, RNN_HEAD_DIM, RNN_HEAD_DIM), RNN_HEAD_DIM ** -0.5)
    gate_a_b = nrm(ks[7], (RNN_WIDTH,), 0.01)
    gate_x_w = nrm(ks[8], (RNN_HEADS, RNN_HEAD_DIM, RNN_HEAD_DIM), RNN_HEAD_DIM ** -0.5)
    gate_x_b = nrm(ks[9], (RNN_WIDTH,), 0.01)
    a_c = jax.random.uniform(ks[10], (RNN_WIDTH,), f32, 0.9, 0.999)
    a0 = a_c ** (1.0 / LRU_C)
    lru_lambda = jnp.log(a0) - jnp.log1p(-a0)
    norm_attn_g = 1.0 + nrm(ks[11], (ATTN_WIDTH,), 0.02)
    norm_rnn_g = 1.0 + nrm(ks[12], (RNN_WIDTH,), 0.02)
    w_out = nrm(ks[13], (MIX_WIDTH, D_MODEL), BETA * MIX_WIDTH ** -0.5)
    b_out = nrm(ks[14], (D_MODEL,), 0.01)
    ln1_g = 1.0 + nrm(ks[15], (D_MODEL,), 0.02)
    ln1_b = nrm(ks[16], (D_MODEL,), 0.01)
    peer_w_q = nrm(ks[17], (D_MODEL, PEER_HEADS * 2 * PEER_HALF_DIM), D_MODEL ** -0.5)
    peer_keys_1 = nrm(ks[18], (PEER_N_KEYS, PEER_HALF_DIM), PEER_HALF_DIM ** -0.5)
    peer_keys_2 = nrm(ks[19], (PEER_N_KEYS, PEER_HALF_DIM), PEER_HALF_DIM ** -0.5)
    peer_u = nrm(ks[20], (PEER_N_EXPERTS, D_MODEL), D_MODEL ** -0.5)
    peer_v = nrm(ks[21], (PEER_N_EXPERTS, D_MODEL), BETA * PEER_HEADS ** -0.5)
    ln2_g = 1.0 + nrm(ks[22], (D_MODEL,), 0.02)
    ln2_b = nrm(ks[23], (D_MODEL,), 0.01)
    return {"x": x, "w_in": w_in, "b_in": b_in, "attn_sinks": attn_sinks,
            "conv_w": conv_w, "conv_b": conv_b, "gate_a_w": gate_a_w, "gate_a_b": gate_a_b,
            "gate_x_w": gate_x_w, "gate_x_b": gate_x_b, "lru_lambda": lru_lambda,
            "norm_attn_g": norm_attn_g, "norm_rnn_g": norm_rnn_g, "w_out": w_out, "b_out": b_out,
            "ln1_g": ln1_g, "ln1_b": ln1_b, "peer_w_q": peer_w_q, "peer_keys_1": peer_keys_1,
            "peer_keys_2": peer_keys_2, "peer_u": peer_u, "peer_v": peer_v,
            "ln2_g": ln2_g, "ln2_b": ln2_b}


def reference(x, w_in, b_in, attn_sinks, conv_w, conv_b, gate_a_w, gate_a_b, gate_x_w, gate_x_b,
              lru_lambda, norm_attn_g, norm_rnn_g, w_out, b_out, ln1_g, ln1_b, peer_w_q,
              peer_keys_1, peer_keys_2, peer_u, peer_v, ln2_g, ln2_b):
    B, S, _ = x.shape
    pos = jnp.arange(S, dtype=jnp.float32)
    inv_freq = ROPE_THETA ** (-jnp.arange(0, ROPE_DIM, 2, dtype=jnp.float32) / ROPE_DIM)
    ang = pos[:, None] * inv_freq[None, :]
    cos = jnp.cos(ang)[None, :, None, :]
    sin = jnp.sin(ang)[None, :, None, :]

    h = x
    for _ in range(DEPTH):
        z = h @ w_in + b_in
        c0 = ATTN_WIDTH
        c1 = c0 + KV_WIDTH
        c2 = c1 + KV_WIDTH
        c3 = c2 + RNN_WIDTH
        q = z[..., :c0].reshape(B, S, ATTN_HEADS, HEAD_DIM)
        k = z[..., c0:c1].reshape(B, S, KV_HEADS, HEAD_DIM)
        v = z[..., c1:c2].reshape(B, S, KV_HEADS, HEAD_DIM)
        q = partial_rotary(q, cos, sin)
        k = partial_rotary(k, cos, sin)
        attn_out = sliding_window_attention(q, k, v, attn_sinks)
        rnn_out = rg_lru_branch(z[..., c2:c3], z[..., c3:], conv_w, conv_b, gate_a_w, gate_a_b,
                                gate_x_w, gate_x_b, lru_lambda)
        mixed = jnp.concatenate([rms_norm(attn_out, norm_attn_g), rms_norm(rnn_out, norm_rnn_g)], axis=-1)
        h = layer_norm(ALPHA * h + (mixed @ w_out + b_out), ln1_g, ln1_b)
        h = layer_norm(ALPHA * h + peer_ffn(h, peer_w_q, peer_keys_1, peer_keys_2, peer_u, peer_v), ln2_g, ln2_b)
    return h
```

```python
import functools
import math

import jax
import jax.numpy as jnp
from jax import lax
from jax.experimental import pallas as pl
from jax.experimental.pallas import tpu as pltpu

F32 = jnp.float32
BF16 = jnp.bfloat16

ATTN_HEADS = 8
KV_HEADS = 2
HEAD_DIM = 64
GQA_GROUP = ATTN_HEADS // KV_HEADS
ATTN_WIDTH = ATTN_HEADS * HEAD_DIM
KV_WIDTH = KV_HEADS * HEAD_DIM
WINDOW = 128
ATTN_BLOCK = 128
ROPE_DIM = HEAD_DIM // 4
ROPE_THETA = 500000.0
RNN_HEADS = 8
CONV_WIDTH = 4
LRU_C = 8.0
PEER_HEADS = 8
PEER_N_KEYS = 128
PEER_HALF_DIM = 128
PEER_TOPK = 16
DEPTH = 1
ALPHA = (2.0 * DEPTH) ** 0.25
EPS = 1e-5

LANES = 128
SUBLANES = 8
VMEM_LIMIT_BYTES = 56 * 1024 * 1024


def _params(*semantics):
    return pltpu.CompilerParams(dimension_semantics=semantics, vmem_limit_bytes=VMEM_LIMIT_BYTES)


def _gelu(x):
    c = math.sqrt(2.0 / math.pi)
    return 0.5 * x * (1.0 + jnp.tanh(c * (x + 0.044715 * (x * x * x))))


def _layer_norm(x, g, b):
    mu = jnp.mean(x, axis=-1, keepdims=True)
    xc = x - mu
    var = jnp.mean(xc * xc, axis=-1, keepdims=True)
    return xc * lax.rsqrt(var + EPS) * g + b


def _rms_norm(x, g):
    return x * lax.rsqrt(jnp.mean(x * x, axis=-1, keepdims=True) + EPS) * g


def _in_proj_kernel(x_ref, w_ref, b_ref, q_ref, k_ref, v_ref, xr_ref, g_ref):
    z = jnp.dot(x_ref[...].astype(BF16), w_ref[...], preferred_element_type=F32) + b_ref[...]
    c0 = q_ref.shape[1]
    c1 = c0 + k_ref.shape[1]
    c2 = c1 + v_ref.shape[1]
    c3 = c2 + xr_ref.shape[1]
    q_ref[...] = z[:, :c0]
    k_ref[...] = z[:, c0:c1]
    v_ref[...] = z[:, c1:c2]
    xr_ref[...] = z[:, c2:c3]
    g_ref[...] = z[:, c3:]


def _in_proj(x2, w_in, b_in, rnn_width, tm=512):
    T, D = x2.shape
    n_in = w_in.shape[1]
    widths = (ATTN_WIDTH, KV_WIDTH, KV_WIDTH, rnn_width, rnn_width)
    row = lambda i: (i, 0)
    fixed = lambda i: (0, 0)
    return pl.pallas_call(
        _in_proj_kernel,
        grid=(T // tm,),
        in_specs=[pl.BlockSpec((tm, D), row), pl.BlockSpec((D, n_in), fixed), pl.BlockSpec((1, n_in), fixed)],
        out_specs=[pl.BlockSpec((tm, w), row) for w in widths],
        out_shape=[jax.ShapeDtypeStruct((T, w), F32) for w in widths],
        compiler_params=_params("parallel"),
        name="in_proj",
    )(x2, w_in.astype(BF16), b_in.reshape(1, n_in))


def _rotary(t, cos_t, sin_lo, sin_hi):
    half = ROPE_DIM // 2
    cols = []
    for c in range(t.shape[1] // LANES):
        tc = t[:, c * LANES:(c + 1) * LANES]
        cols.append(tc * cos_t + pltpu.roll(tc, LANES - half, axis=1) * sin_lo + pltpu.roll(tc, half, axis=1) * sin_hi)
    return cols[0] if len(cols) == 1 else jnp.concatenate(cols, axis=1)


def _attention_kernel(sink_ref, q_ref, kc_ref, kp_ref, vc_ref, vp_ref, cos_ref, slo_ref, shi_ref,
                      cosp_ref, slop_ref, ship_ref, g_ref, o_ref):
    n = pl.program_id(1)
    blk = ATTN_BLOCK
    q = _rotary(q_ref[0], cos_ref[...], slo_ref[...], shi_ref[...]) * (HEAD_DIM ** -0.5)
    kc = _rotary(kc_ref[0], cos_ref[...], slo_ref[...], shi_ref[...])
    kp = _rotary(kp_ref[0], cosp_ref[...], slop_ref[...], ship_ref[...])
    kw = jnp.concatenate([kp, kc], axis=0).astype(BF16)
    vw = jnp.concatenate([vp_ref[0], vc_ref[0]], axis=0).astype(BF16)
    qb = q.astype(BF16)

    qi = lax.broadcasted_iota(jnp.int32, (blk, 2 * blk), 0)
    ki = lax.broadcasted_iota(jnp.int32, (blk, 2 * blk), 1)
    diff = blk + qi - ki
    first_key = jnp.where(n == 0, blk, 0)
    valid = (diff >= 0) & (diff < WINDOW) & (ki >= first_key)

    outs = []
    for h in range(ATTN_HEADS):
        kv = h // GQA_GROUP
        qh = qb[:, h * HEAD_DIM:(h + 1) * HEAD_DIM]
        kh = kw[:, kv * HEAD_DIM:(kv + 1) * HEAD_DIM]
        vh = vw[:, kv * HEAD_DIM:(kv + 1) * HEAD_DIM]
        s = lax.dot_general(qh, kh, (((1,), (1,)), ((), ())), preferred_element_type=F32)
        s = jnp.where(valid, s, -jnp.inf)
        sink = sink_ref[h]
        m = jnp.maximum(jnp.max(s, axis=-1, keepdims=True), sink)
        p = jnp.exp(s - m)
        denom = jnp.sum(p, axis=-1, keepdims=True) + jnp.exp(sink - m)
        probs = (p / denom).astype(BF16)
        outs.append(jnp.dot(probs, vh, preferred_element_type=F32))
    o = jnp.concatenate(outs, axis=1)
    o_ref[0] = _rms_norm(o, g_ref[...])


def _rope_tables(S):
    half = ROPE_DIM // 2
    pos = jnp.arange(S, dtype=F32)
    inv_freq = ROPE_THETA ** (-jnp.arange(0, ROPE_DIM, 2, dtype=F32) / ROPE_DIM)
    ang = pos[:, None] * inv_freq[None, :]
    cos, sin = jnp.cos(ang), jnp.sin(ang)
    ones = jnp.ones((S, HEAD_DIM - ROPE_DIM), F32)
    zeros = jnp.zeros((S, HEAD_DIM - ROPE_DIM), F32)
    zh = jnp.zeros((S, half), F32)
    per_head = lambda *parts: jnp.tile(jnp.concatenate(parts, axis=1), (1, LANES // HEAD_DIM))
    return per_head(cos, cos, ones), per_head(-sin, zh, zeros), per_head(zh, sin, zeros)


def _attention(q, k, v, sinks, norm_g, B, S):
    blk = ATTN_BLOCK
    nb = S // blk
    cos_t, sin_lo, sin_hi = _rope_tables(S)
    cur = lambda b, n: (b, n, 0)
    prev = lambda b, n: (b, jnp.maximum(n - 1, 0), 0)
    tcur = lambda b, n: (n, 0)
    tprev = lambda b, n: (jnp.maximum(n - 1, 0), 0)
    tab = lambda im: pl.BlockSpec((blk, LANES), im)
    return pl.pallas_call(
        _attention_kernel,
        grid=(B, nb),
        in_specs=[pl.BlockSpec(memory_space=pltpu.SMEM),
                  pl.BlockSpec((1, blk, ATTN_WIDTH), cur),
                  pl.BlockSpec((1, blk, KV_WIDTH), cur), pl.BlockSpec((1, blk, KV_WIDTH), prev),
                  pl.BlockSpec((1, blk, KV_WIDTH), cur), pl.BlockSpec((1, blk, KV_WIDTH), prev),
                  tab(tcur), tab(tcur), tab(tcur), tab(tprev), tab(tprev), tab(tprev),
                  pl.BlockSpec((1, ATTN_WIDTH), lambda b, n: (0, 0))],
        out_specs=pl.BlockSpec((1, blk, ATTN_WIDTH), cur),
        out_shape=jax.ShapeDtypeStruct((B, S, ATTN_WIDTH), F32),
        compiler_params=_params("parallel", "parallel"),
        name="attention",
    )(sinks, q.reshape(B, S, ATTN_WIDTH), k.reshape(B, S, KV_WIDTH), k.reshape(B, S, KV_WIDTH),
      v.reshape(B, S, KV_WIDTH), v.reshape(B, S, KV_WIDTH),
      cos_t, sin_lo, sin_hi, cos_t, sin_lo, sin_hi, norm_g.reshape(1, ATTN_WIDTH))


def _rglru_kernel(xr_ref, gin_ref, cw_ref, cb_ref, wg_ref, bg_ref, lam_ref, g_ref, o_ref, tail_ref, h_ref):
    ts, C = xr_ref.shape[1], xr_ref.shape[2]

    @pl.when(pl.program_id(1) == 0)
    def _():
        tail_ref[...] = jnp.zeros_like(tail_ref)
        h_ref[...] = jnp.zeros_like(h_ref)

    xr = xr_ref[0]
    xcat = jnp.concatenate([tail_ref[...], xr], axis=0)
    tail_ref[...] = xr[ts - SUBLANES:, :]
    xc = cb_ref[...] + cw_ref[CONV_WIDTH - 1:CONV_WIDTH, :] * xr
    for j in range(CONV_WIDTH - 1):
        back = CONV_WIDTH - 1 - j
        xc = xc + cw_ref[j:j + 1, :] * pltpu.roll(xcat, back, axis=0)[SUBLANES:, :]

    gates = jnp.dot(xc.astype(BF16), wg_ref[...], preferred_element_type=F32) + bg_ref[...]
    r = jax.nn.sigmoid(gates[:, :C])
    i = jax.nn.sigmoid(gates[:, C:])
    lam = lam_ref[...]
    softplus_neg_lam = jnp.maximum(-lam, 0.0) + jnp.log1p(jnp.exp(-jnp.abs(lam)))
    log_a = -LRU_C * r * softplus_neg_lam
    a = jnp.exp(log_a)
    b = jnp.sqrt((1.0 + jnp.exp(2.0 * log_a)) * jnp.tanh(-log_a)) * (i * xc)

    row = lax.broadcasted_iota(jnp.int32, (ts, C), 0)
    d = 1
    while d < ts:
        keep = row >= d
        b = jnp.where(keep, a * pltpu.roll(b, d, axis=0) + b, b)
        a = jnp.where(keep, a * pltpu.roll(a, d, axis=0), a)
        d *= 2
    h = b + a * h_ref[0:1, :]
    h_ref[...] = jnp.broadcast_to(h[ts - 1:ts, :], h_ref.shape)
    o_ref[0] = _rms_norm(_gelu(gin_ref[0]) * h, g_ref[...])


def _block_diag(w):
    H, a, b = w.shape
    eye = jnp.eye(H, dtype=w.dtype)
    return (w[:, :, None, :] * eye[:, None, :, None]).reshape(H * a, H * b)


def _rglru(xr, gate_in, conv_w, conv_b, gate_a_w, gate_a_b, gate_x_w, gate_x_b, lru_lambda, norm_g, B, S, ts=256):
    C = xr.shape[-1]
    wg = jnp.concatenate([_block_diag(gate_a_w), _block_diag(gate_x_w)], axis=1).astype(BF16)
    bg = jnp.concatenate([gate_a_b, gate_x_b]).reshape(1, 2 * C)
    cur = lambda b, n: (b, n, 0)
    fixed = lambda b, n: (0, 0)
    return pl.pallas_call(
        _rglru_kernel,
        grid=(B, S // ts),
        in_specs=[pl.BlockSpec((1, ts, C), cur), pl.BlockSpec((1, ts, C), cur),
                  pl.BlockSpec((CONV_WIDTH, C), fixed), pl.BlockSpec((1, C), fixed),
                  pl.BlockSpec((C, 2 * C), fixed), pl.BlockSpec((1, 2 * C), fixed),
                  pl.BlockSpec((1, C), fixed), pl.BlockSpec((1, C), fixed)],
        out_specs=pl.BlockSpec((1, ts, C), cur),
        out_shape=jax.ShapeDtypeStruct((B, S, C), F32),
        scratch_shapes=[pltpu.VMEM((SUBLANES, C), F32), pltpu.VMEM((SUBLANES, C), F32)],
        compiler_params=_params("parallel", "arbitrary"),
        name="rglru",
    )(xr.reshape(B, S, C), gate_in.reshape(B, S, C), conv_w, conv_b.reshape(1, C), wg, bg,
      lru_lambda.reshape(1, C), norm_g.reshape(1, C))


def _post_mix_kernel(x_ref, ma_ref, mr_ref, woa_ref, wor_ref, bo_ref, g1_ref, b1_ref, wq_ref, k1_ref, k2_ref,
                     h_ref, s_ref):
    y = (jnp.dot(ma_ref[...].astype(BF16), woa_ref[...], preferred_element_type=F32)
         + jnp.dot(mr_ref[...].astype(BF16), wor_ref[...], preferred_element_type=F32) + bo_ref[...])
    h = _layer_norm(ALPHA * x_ref[...] + y, g1_ref[...], b1_ref[...])
    h_ref[...] = h
    q = jnp.dot(h.astype(BF16), wq_ref[...], preferred_element_type=F32).astype(BF16)
    nt = (((1,), (1,)), ((), ()))
    for hd in range(PEER_HEADS):
        for half, k_ref in enumerate((k1_ref, k2_ref)):
            c = (hd * 2 + half) * PEER_HALF_DIM
            s_ref[hd * 2 + half] = lax.dot_general(k_ref[...], q[:, c:c + PEER_HALF_DIM], nt,
                                                   preferred_element_type=F32)


def _post_mix(x2, mixed_attn, mixed_rnn, w_out, b_out, ln1_g, ln1_b, peer_w_q, keys_1, keys_2, tm=256):
    T, D = x2.shape
    wa, wr = mixed_attn.shape[1], mixed_rnn.shape[1]
    nq = peer_w_q.shape[1]
    row = lambda i: (i, 0)
    fixed = lambda i: (0, 0)
    full = lambda a: pl.BlockSpec(a.shape, fixed)
    wo = w_out.astype(BF16)
    woa, wor = wo[:wa], wo[wa:]
    wq = peer_w_q.astype(BF16)
    k1, k2 = keys_1.astype(BF16), keys_2.astype(BF16)
    vec = lambda a: a.reshape(1, -1)
    return pl.pallas_call(
        _post_mix_kernel,
        grid=(T // tm,),
        in_specs=[pl.BlockSpec((tm, D), row), pl.BlockSpec((tm, wa), row), pl.BlockSpec((tm, wr), row),
                  full(woa), full(wor), pl.BlockSpec((1, D), fixed), pl.BlockSpec((1, D), fixed),
                  pl.BlockSpec((1, D), fixed), full(wq), full(k1), full(k2)],
        out_specs=[pl.BlockSpec((tm, D), row),
                   pl.BlockSpec((2 * PEER_HEADS, PEER_N_KEYS, tm), lambda i: (0, 0, i))],
        out_shape=[jax.ShapeDtypeStruct((T, D), F32),
                   jax.ShapeDtypeStruct((2 * PEER_HEADS, PEER_N_KEYS, T), F32)],
        compiler_params=_params("parallel"),
        name="post_mix",
    )(x2, mixed_attn, mixed_rnn, woa, wor, vec(b_out), vec(ln1_g), vec(ln1_b), wq, k1, k2)


def _top_rows(s, k, payload=None):
    rows, n = s.shape
    row = lax.broadcasted_iota(jnp.int32, s.shape, 0)
    out_row = lax.broadcasted_iota(jnp.int32, (k, n), 0)
    vals = jnp.zeros((k, n), s.dtype)
    picks = jnp.zeros((k, n), jnp.int32)
    for j in range(k):
        m = jnp.max(s, axis=0, keepdims=True)
        idx = jnp.min(jnp.where(s == m, row, rows), axis=0, keepdims=True)
        hit = row == idx
        pick = idx if payload is None else jnp.max(jnp.where(hit, payload, -1), axis=0, keepdims=True)
        vals = jnp.where(out_row == j, m, vals)
        picks = jnp.where(out_row == j, pick, picks)
        s = jnp.where(hit, -jnp.inf, s)
    return vals, picks


def _peer_topk_kernel(s_ref, e_ref, g_ref):
    K = PEER_TOPK
    experts, gates = [], []
    for hd in range(PEER_HEADS):
        v1, i1 = _top_rows(s_ref[2 * hd], K)
        v2, i2 = _top_rows(s_ref[2 * hd + 1], K)
        cand = jnp.concatenate([v1[i:i + 1] + v2 for i in range(K)], axis=0)
        ids = jnp.concatenate([i1[i:i + 1] * PEER_N_KEYS + i2 for i in range(K)], axis=0)
        sc, ex = _top_rows(cand, K, payload=ids)
        p = jnp.exp(sc - jnp.max(sc, axis=0, keepdims=True))
        gates.append(p / jnp.sum(p, axis=0, keepdims=True))
        experts.append(ex)
    e_ref[...] = jnp.concatenate(experts, axis=0).T
    g_ref[...] = jnp.concatenate(gates, axis=0).T


def _peer_topk(scores_t, tl=256):
    n_sets, n_keys, T = scores_t.shape
    E = PEER_HEADS * PEER_TOPK
    return pl.pallas_call(
        _peer_topk_kernel,
        grid=(T // tl,),
        in_specs=[pl.BlockSpec((n_sets, n_keys, tl), lambda i: (0, 0, i))],
        out_specs=[pl.BlockSpec((tl, E), lambda i: (i, 0)), pl.BlockSpec((tl, E), lambda i: (i, 0))],
        out_shape=[jax.ShapeDtypeStruct((T, E), jnp.int32), jax.ShapeDtypeStruct((T, E), F32)],
        compiler_params=_params("parallel"),
        name="peer_topk",
    )(scores_t)


def _row_copy(table_hbm, buf, slot, e, r, sem):
    return pltpu.make_async_copy(table_hbm.at[pl.ds(e, 1), :], buf.at[slot, pl.ds(r, 1), :], sem.at[slot])


def _peer_apply_kernel(ex_ref, exn_ref, g_ref, h_ref, ln_g_ref, ln_b_ref, u_hbm, v_hbm, o_ref, ubuf, vbuf, usem, vsem):
    i = pl.program_id(0)
    n = pl.num_programs(0)
    tk, E = ex_ref.shape
    rows = tk * E
    slot = lax.rem(i, 2)

    def issue(idx_ref, dst_slot):
        for t in range(tk):
            def body(j, carry, t=t):
                e = idx_ref[t, j]
                _row_copy(u_hbm, ubuf, dst_slot, e, t * E + j, usem).start()
                _row_copy(v_hbm, vbuf, dst_slot, e, t * E + j, vsem).start()
                return carry
            lax.fori_loop(0, E, body, 0, unroll=8)

    @pl.when(i == 0)
    def _():
        issue(ex_ref, slot)

    @pl.when(i + 1 < n)
    def _():
        issue(exn_ref, 1 - slot)

    pltpu.make_async_copy(ubuf.at[slot], ubuf.at[slot], usem.at[slot]).wait()
    pltpu.make_async_copy(vbuf.at[slot], vbuf.at[slot], vsem.at[slot]).wait()

    eye = lax.broadcasted_iota(jnp.int32, (E, E), 0) == lax.broadcasted_iota(jnp.int32, (E, E), 1)
    outs = []
    for t in range(tk):
        h = h_ref[t:t + 1, :]
        u = ubuf[slot, t * E:(t + 1) * E, :]
        act = jnp.sum(u * h, axis=1, keepdims=True)
        gate = jnp.sum(jnp.where(eye, g_ref[t:t + 1, :], 0.0), axis=1, keepdims=True)
        w = gate * _gelu(act)
        v = vbuf[slot, t * E:(t + 1) * E, :]
        outs.append(jnp.sum(w * v, axis=0, keepdims=True))
    y = jnp.concatenate(outs, axis=0)
    o_ref[...] = _layer_norm(ALPHA * h_ref[...] + y, ln_g_ref[...], ln_b_ref[...])


def _peer_apply(h1, experts, gates, peer_u, peer_v, ln2_g, ln2_b, tk=8):
    T, D = h1.shape
    E = experts.shape[1]
    n = T // tk
    row = lambda i: (i, 0)
    nxt = lambda i: (jnp.minimum(i + 1, n - 1), 0)
    fixed = lambda i: (0, 0)
    return pl.pallas_call(
        _peer_apply_kernel,
        grid=(n,),
        in_specs=[pl.BlockSpec((tk, E), row, memory_space=pltpu.SMEM),
                  pl.BlockSpec((tk, E), nxt, memory_space=pltpu.SMEM),
                  pl.BlockSpec((tk, E), row), pl.BlockSpec((tk, D), row),
                  pl.BlockSpec((1, D), fixed), pl.BlockSpec((1, D), fixed),
                  pl.BlockSpec(memory_space=pl.ANY), pl.BlockSpec(memory_space=pl.ANY)],
        out_specs=pl.BlockSpec((tk, D), row),
        out_shape=jax.ShapeDtypeStruct((T, D), F32),
        scratch_shapes=[pltpu.VMEM((2, tk * E, D), F32), pltpu.VMEM((2, tk * E, D), F32),
                        pltpu.SemaphoreType.DMA((2,)), pltpu.SemaphoreType.DMA((2,))],
        compiler_params=_params("arbitrary"),
        name="peer_apply",
    )(experts, experts, gates, h1, ln2_g.reshape(1, D), ln2_b.reshape(1, D), peer_u, peer_v)


def kernel(x, w_in, b_in, attn_sinks, conv_w, conv_b, gate_a_w, gate_a_b, gate_x_w, gate_x_b, lru_lambda, norm_attn_g, norm_rnn_g, w_out, b_out, ln1_g, ln1_b, peer_w_q, peer_keys_1, peer_keys_2, peer_u, peer_v, ln2_g, ln2_b):
    B, S, D = x.shape
    x2 = x.reshape(B * S, D)
    rnn_width = conv_w.shape[1]
    q, k, v, xr, gate_in = _in_proj(x2, w_in, b_in, rnn_width)
    mixed_attn = _attention(q, k, v, attn_sinks, norm_attn_g, B, S).reshape(B * S, ATTN_WIDTH)
    mixed_rnn = _rglru(xr, gate_in, conv_w, conv_b, gate_a_w, gate_a_b, gate_x_w, gate_x_b, lru_lambda,
                       norm_rnn_g, B, S).reshape(B * S, rnn_width)
    h1, scores_t = _post_mix(x2, mixed_attn, mixed_rnn, w_out, b_out, ln1_g, ln1_b, peer_w_q,
                             peer_keys_1, peer_keys_2)
    experts, gates = _peer_topk(scores_t)
    out = _peer_apply(h1, experts, gates, peer_u, peer_v, ln2_g, ln2_b)
    return out.reshape(B, S, D)
```

```python
import functools
import math

import jax
import jax.numpy as jnp
from jax import lax
from jax.experimental import pallas as pl
from jax.experimental.pallas import tpu as pltpu

F32 = jnp.float32
BF16 = jnp.bfloat16

ATTN_HEADS = 8
KV_HEADS = 2
HEAD_DIM = 64
GQA_GROUP = ATTN_HEADS // KV_HEADS
ATTN_WIDTH = ATTN_HEADS * HEAD_DIM
KV_WIDTH = KV_HEADS * HEAD_DIM
WINDOW = 128
ATTN_BLOCK = 128
ROPE_DIM = HEAD_DIM // 4
ROPE_THETA = 500000.0
RNN_HEADS = 8
CONV_WIDTH = 4
LRU_C = 8.0
PEER_HEADS = 8
PEER_N_KEYS = 128
PEER_HALF_DIM = 128
PEER_TOPK = 16
DEPTH = 1
ALPHA = (2.0 * DEPTH) ** 0.25
EPS = 1e-5

LANES = 128
SUBLANES = 8
VMEM_LIMIT_BYTES = 56 * 1024 * 1024


def _params(*semantics):
    return pltpu.CompilerParams(dimension_semantics=semantics, vmem_limit_bytes=VMEM_LIMIT_BYTES)


def _gelu(x):
    c = math.sqrt(2.0 / math.pi)
    return 0.5 * x * (1.0 + jnp.tanh(c * (x + 0.044715 * (x * x * x))))


def _layer_norm(x, g, b):
    mu = jnp.mean(x, axis=-1, keepdims=True)
    xc = x - mu
    var = jnp.mean(xc * xc, axis=-1, keepdims=True)
    return xc * lax.rsqrt(var + EPS) * g + b


def _rms_norm(x, g):
    return x * lax.rsqrt(jnp.mean(x * x, axis=-1, keepdims=True) + EPS) * g


def _in_proj_kernel(x_ref, w_ref, b_ref, q_ref, k_ref, v_ref, xr_ref, g_ref):
    z = jnp.dot(x_ref[...].astype(BF16), w_ref[...], preferred_element_type=F32) + b_ref[...]
    c0 = q_ref.shape[1]
    c1 = c0 + k_ref.shape[1]
    c2 = c1 + v_ref.shape[1]
    c3 = c2 + xr_ref.shape[1]
    q_ref[...] = z[:, :c0]
    k_ref[...] = z[:, c0:c1]
    v_ref[...] = z[:, c1:c2]
    xr_ref[...] = z[:, c2:c3]
    g_ref[...] = z[:, c3:]


def _in_proj(x2, w_in, b_in, rnn_width, tm=512):
    T, D = x2.shape
    n_in = w_in.shape[1]
    widths = (ATTN_WIDTH, KV_WIDTH, KV_WIDTH, rnn_width, rnn_width)
    row = lambda i: (i, 0)
    fixed = lambda i: (0, 0)
    return pl.pallas_call(
        _in_proj_kernel,
        grid=(T // tm,),
        in_specs=[pl.BlockSpec((tm, D), row), pl.BlockSpec((D, n_in), fixed), pl.BlockSpec((1, n_in), fixed)],
        out_specs=[pl.BlockSpec((tm, w), row) for w in widths],
        out_shape=[jax.ShapeDtypeStruct((T, w), F32) for w in widths],
        compiler_params=_params("parallel"),
        name="in_proj",
    )(x2, w_in.astype(BF16), b_in.reshape(1, n_in))


def _rotary(t, cos_t, sin_lo, sin_hi):
    half = ROPE_DIM // 2
    cols = []
    for c in range(t.shape[1] // LANES):
        tc = t[:, c * LANES:(c + 1) * LANES]
        cols.append(tc * cos_t + pltpu.roll(tc, LANES - half, axis=1) * sin_lo + pltpu.roll(tc, half, axis=1) * sin_hi)
    return cols[0] if len(cols) == 1 else jnp.concatenate(cols, axis=1)


def _attention_kernel(sink_ref, q_ref, kc_ref, kp_ref, vc_ref, vp_ref, cos_ref, slo_ref, shi_ref,
                      cosp_ref, slop_ref, ship_ref, g_ref, o_ref):
    n = pl.program_id(1)
    blk = ATTN_BLOCK
    q = _rotary(q_ref[0], cos_ref[...], slo_ref[...], shi_ref[...]) * (HEAD_DIM ** -0.5)
    kc = _rotary(kc_ref[0], cos_ref[...], slo_ref[...], shi_ref[...])
    kp = _rotary(kp_ref[0], cosp_ref[...], slop_ref[...], ship_ref[...])
    kw = jnp.concatenate([kp, kc], axis=0).astype(BF16)
    vw = jnp.concatenate([vp_ref[0], vc_ref[0]], axis=0).astype(BF16)
    qb = q.astype(BF16)

    qi = lax.broadcasted_iota(jnp.int32, (blk, 2 * blk), 0)
    ki = lax.broadcasted_iota(jnp.int32, (blk, 2 * blk), 1)
    diff = blk + qi - ki
    first_key = jnp.where(n == 0, blk, 0)
    valid = (diff >= 0) & (diff < WINDOW) & (ki >= first_key)

    outs = []
    for h in range(ATTN_HEADS):
        kv = h // GQA_GROUP
        qh = qb[:, h * HEAD_DIM:(h + 1) * HEAD_DIM]
        kh = kw[:, kv * HEAD_DIM:(kv + 1) * HEAD_DIM]
        vh = vw[:, kv * HEAD_DIM:(kv + 1) * HEAD_DIM]
        s = lax.dot_general(qh, kh, (((1,), (1,)), ((), ())), preferred_element_type=F32)
        s = jnp.where(valid, s, -jnp.inf)
        sink = sink_ref[h]
        m = jnp.maximum(jnp.max(s, axis=-1, keepdims=True), sink)
        p = jnp.exp(s - m)
        denom = jnp.sum(p, axis=-1, keepdims=True) + jnp.exp(sink - m)
        probs = (p / denom).astype(BF16)
        outs.append(jnp.dot(probs, vh, preferred_element_type=F32))
    o = jnp.concatenate(outs, axis=1)
    o_ref[0] = _rms_norm(o, g_ref[...])


def _rope_tables(S):
    half = ROPE_DIM // 2
    pos = jnp.arange(S, dtype=F32)
    inv_freq = ROPE_THETA ** (-jnp.arange(0, ROPE_DIM, 2, dtype=F32) / ROPE_DIM)
    ang = pos[:, None] * inv_freq[None, :]
    cos, sin = jnp.cos(ang), jnp.sin(ang)
    ones = jnp.ones((S, HEAD_DIM - ROPE_DIM), F32)
    zeros = jnp.zeros((S, HEAD_DIM - ROPE_DIM), F32)
    zh = jnp.zeros((S, half), F32)
    per_head = lambda *parts: jnp.tile(jnp.concatenate(parts, axis=1), (1, LANES // HEAD_DIM))
    return per_head(cos, cos, ones), per_head(-sin, zh, zeros), per_head(zh, sin, zeros)


def _attention(q, k, v, sinks, norm_g, B, S):
    blk = ATTN_BLOCK
    nb = S // blk
    cos_t, sin_lo, sin_hi = _rope_tables(S)
    cur = lambda b, n: (b, n, 0)
    prev = lambda b, n: (b, jnp.maximum(n - 1, 0), 0)
    tcur = lambda b, n: (n, 0)
    tprev = lambda b, n: (jnp.maximum(n - 1, 0), 0)
    tab = lambda im: pl.BlockSpec((blk, LANES), im)
    return pl.pallas_call(
        _attention_kernel,
        grid=(B, nb),
        in_specs=[pl.BlockSpec(memory_space=pltpu.SMEM),
                  pl.BlockSpec((1, blk, ATTN_WIDTH), cur),
                  pl.BlockSpec((1, blk, KV_WIDTH), cur), pl.BlockSpec((1, blk, KV_WIDTH), prev),
                  pl.BlockSpec((1, blk, KV_WIDTH), cur), pl.BlockSpec((1, blk, KV_WIDTH), prev),
                  tab(tcur), tab(tcur), tab(tcur), tab(tprev), tab(tprev), tab(tprev),
                  pl.BlockSpec((1, ATTN_WIDTH), lambda b, n: (0, 0))],
        out_specs=pl.BlockSpec((1, blk, ATTN_WIDTH), cur),
        out_shape=jax.ShapeDtypeStruct((B, S, ATTN_WIDTH), F32),
        compiler_params=_params("parallel", "parallel"),
        name="attention",
    )(sinks, q.reshape(B, S, ATTN_WIDTH), k.reshape(B, S, KV_WIDTH), k.reshape(B, S, KV_WIDTH),
      v.reshape(B, S, KV_WIDTH), v.reshape(B, S, KV_WIDTH),
      cos_t, sin_lo, sin_hi, cos_t, sin_lo, sin_hi, norm_g.reshape(1, ATTN_WIDTH))


def _rglru_kernel(xr_ref, gin_ref, cw_ref, cb_ref, wg_ref, bg_ref, lam_ref, g_ref, o_ref, tail_ref, h_ref):
    ts, C = xr_ref.shape[1], xr_ref.shape[2]

    @pl.when(pl.program_id(1) == 0)
    def _():
        tail_ref[...] = jnp.zeros_like(tail_ref)
        h_ref[...] = jnp.zeros_like(h_ref)

    xr = xr_ref[0]
    xcat = jnp.concatenate([tail_ref[...], xr], axis=0)
    tail_ref[...] = xr[ts - SUBLANES:, :]
    xc = cb_ref[...] + cw_ref[CONV_WIDTH - 1:CONV_WIDTH, :] * xr
    for j in range(CONV_WIDTH - 1):
        back = CONV_WIDTH - 1 - j
        xc = xc + cw_ref[j:j + 1, :] * pltpu.roll(xcat, back, axis=0)[SUBLANES:, :]

    gates = jnp.dot(xc.astype(BF16), wg_ref[...], preferred_element_type=F32) + bg_ref[...]
    r = jax.nn.sigmoid(gates[:, :C])
    i = jax.nn.sigmoid(gates[:, C:])
    lam = lam_ref[...]
    softplus_neg_lam = jnp.maximum(-lam, 0.0) + jnp.log1p(jnp.exp(-jnp.abs(lam)))
    log_a = -LRU_C * r * softplus_neg_lam
    a = jnp.exp(log_a)
    b = jnp.sqrt((1.0 + jnp.exp(2.0 * log_a)) * jnp.tanh(-log_a)) * (i * xc)

    row = lax.broadcasted_iota(jnp.int32, (ts, C), 0)
    d = 1
    while d < ts:
        keep = row >= d
        b = jnp.where(keep, a * pltpu.roll(b, d, axis=0) + b, b)
        a = jnp.where(keep, a * pltpu.roll(a, d, axis=0), a)
        d *= 2
    h = b + a * h_ref[0:1, :]
    h_ref[...] = jnp.broadcast_to(h[ts - 1:ts, :], h_ref.shape)
    o_ref[0] = _rms_norm(_gelu(gin_ref[0]) * h, g_ref[...])


def _block_diag(w):
    H, a, b = w.shape
    eye = jnp.eye(H, dtype=w.dtype)
    return (w[:, :, None, :] * eye[:, None, :, None]).reshape(H * a, H * b)


def _rglru(xr, gate_in, conv_w, conv_b, gate_a_w, gate_a_b, gate_x_w, gate_x_b, lru_lambda, norm_g, B, S, ts=256):
    C = xr.shape[-1]
    wg = jnp.concatenate([_block_diag(gate_a_w), _block_diag(gate_x_w)], axis=1).astype(BF16)
    bg = jnp.concatenate([gate_a_b, gate_x_b]).reshape(1, 2 * C)
    cur = lambda b, n: (b, n, 0)
    fixed = lambda b, n: (0, 0)
    return pl.pallas_call(
        _rglru_kernel,
        grid=(B, S // ts),
        in_specs=[pl.BlockSpec((1, ts, C), cur), pl.BlockSpec((1, ts, C), cur),
                  pl.BlockSpec((CONV_WIDTH, C), fixed), pl.BlockSpec((1, C), fixed),
                  pl.BlockSpec((C, 2 * C), fixed), pl.BlockSpec((1, 2 * C), fixed),
                  pl.BlockSpec((1, C), fixed), pl.BlockSpec((1, C), fixed)],
        out_specs=pl.BlockSpec((1, ts, C), cur),
        out_shape=jax.ShapeDtypeStruct((B, S, C), F32),
        scratch_shapes=[pltpu.VMEM((SUBLANES, C), F32), pltpu.VMEM((SUBLANES, C), F32)],
        compiler_params=_params("parallel", "arbitrary"),
        name="rglru",
    )(xr.reshape(B, S, C), gate_in.reshape(B, S, C), conv_w, conv_b.reshape(1, C), wg, bg,
      lru_lambda.reshape(1, C), norm_g.reshape(1, C))


def _post_mix_kernel(x_ref, ma_ref, mr_ref, woa_ref, wor_ref, bo_ref, g1_ref, b1_ref, wq_ref, k1_ref, k2_ref,
                     h_ref, s_ref):
    y = (jnp.dot(ma_ref[...].astype(BF16), woa_ref[...], preferred_element_type=F32)
         + jnp.dot(mr_ref[...].astype(BF16), wor_ref[...], preferred_element_type=F32) + bo_ref[...])
    h = _layer_norm(ALPHA * x_ref[...] + y, g1_ref[...], b1_ref[...])
    h_ref[...] = h
    q = jnp.dot(h.astype(BF16), wq_ref[...], preferred_element_type=F32).astype(BF16)
    nt = (((1,), (1,)), ((), ()))
    for hd in range(PEER_HEADS):
        for half, k_ref in enumerate((k1_ref, k2_ref)):
            c = (hd * 2 + half) * PEER_HALF_DIM
            s_ref[hd * 2 + half] = lax.dot_general(k_ref[...], q[:, c:c + PEER_HALF_DIM], nt,
                                                   preferred_element_type=F32)


def _post_mix(x2, mixed_attn, mixed_rnn, w_out, b_out, ln1_g, ln1_b, peer_w_q, keys_1, keys_2, tm=256):
    T, D = x2.shape
    wa, wr = mixed_attn.shape[1], mixed_rnn.shape[1]
    nq = peer_w_q.shape[1]
    row = lambda i: (i, 0)
    fixed = lambda i: (0, 0)
    full = lambda a: pl.BlockSpec(a.shape, fixed)
    wo = w_out.astype(BF16)
    woa, wor = wo[:wa], wo[wa:]
    wq = peer_w_q.astype(BF16)
    k1, k2 = keys_1.astype(BF16), keys_2.astype(BF16)
    vec = lambda a: a.reshape(1, -1)
    return pl.pallas_call(
        _post_mix_kernel,
        grid=(T // tm,),
        in_specs=[pl.BlockSpec((tm, D), row), pl.BlockSpec((tm, wa), row), pl.BlockSpec((tm, wr), row),
                  full(woa), full(wor), pl.BlockSpec((1, D), fixed), pl.BlockSpec((1, D), fixed),
                  pl.BlockSpec((1, D), fixed), full(wq), full(k1), full(k2)],
        out_specs=[pl.BlockSpec((tm, D), row),
                   pl.BlockSpec((2 * PEER_HEADS, PEER_N_KEYS, tm), lambda i: (0, 0, i))],
        out_shape=[jax.ShapeDtypeStruct((T, D), F32),
                   jax.ShapeDtypeStruct((2 * PEER_HEADS, PEER_N_KEYS, T), F32)],
        compiler_params=_params("parallel"),
        name="post_mix",
    )(x2, mixed_attn, mixed_rnn, woa, wor, vec(b_out), vec(ln1_g), vec(ln1_b), wq, k1, k2)


def _top_rows(s, k, payload=None):
    rows, n = s.shape
    row = lax.broadcasted_iota(jnp.int32, s.shape, 0)
    out_row = lax.broadcasted_iota(jnp.int32, (k, n), 0)
    vals = jnp.zeros((k, n), s.dtype)
    picks = jnp.zeros((k, n), jnp.int32)
    for j in range(k):
        m = jnp.max(s, axis=0, keepdims=True)
        idx = jnp.min(jnp.where(s == m, row, rows), axis=0, keepdims=True)
        hit = row == idx
        pick = idx if payload is None else jnp.max(jnp.where(hit, payload, -1), axis=0, keepdims=True)
        vals = jnp.where(out_row == j, m, vals)
        picks = jnp.where(out_row == j, pick, picks)
        s = jnp.where(hit, -jnp.inf, s)
    return vals, picks


def _peer_topk_kernel(s_ref, e_ref, g_ref):
    K = PEER_TOPK
    experts, gates = [], []
    for hd in range(PEER_HEADS):
        v1, i1 = _top_rows(s_ref[2 * hd], K)
        v2, i2 = _top_rows(s_ref[2 * hd + 1], K)
        cand = jnp.concatenate([v1[i:i + 1] + v2 for i in range(K)], axis=0)
        ids = jnp.concatenate([i1[i:i + 1] * PEER_N_KEYS + i2 for i in range(K)], axis=0)
        sc, ex = _top_rows(cand, K, payload=ids)
        p = jnp.exp(sc - jnp.max(sc, axis=0, keepdims=True))
        gates.append(p / jnp.sum(p, axis=0, keepdims=True))
        experts.append(ex)
    e_ref[...] = jnp.concatenate(experts, axis=0).T
    g_ref[...] = jnp.concatenate(gates, axis=0).T


def _peer_topk(scores_t, tl=256):
    n_sets, n_keys, T = scores_t.shape
    E = PEER_HEADS * PEER_TOPK
    return pl.pallas_call(
        _peer_topk_kernel,
        grid=(T // tl,),
        in_specs=[pl.BlockSpec((n_sets, n_keys, tl), lambda i: (0, 0, i))],
        out_specs=[pl.BlockSpec((tl, E), lambda i: (i, 0)), pl.BlockSpec((tl, E), lambda i: (i, 0))],
        out_shape=[jax.ShapeDtypeStruct((T, E), jnp.int32), jax.ShapeDtypeStruct((T, E), F32)],
        compiler_params=_params("parallel"),
        name="peer_topk",
    )(scores_t)


def _sum_over_sublanes(p, lo4, lo2, lo1):
    z = [jnp.where(lo4, p[k], p[k + 4]) + pltpu.roll(jnp.where(lo4, p[k + 4], p[k]), 4, axis=0) for k in range(4)]
    y = [jnp.where(lo2, z[k] + pltpu.roll(z[k], SUBLANES - 2, axis=0), z[k + 2] + pltpu.roll(z[k + 2], 2, axis=0))
         for k in range(2)]
    return jnp.where(lo1, y[0] + pltpu.roll(y[0], SUBLANES - 1, axis=0), y[1] + pltpu.roll(y[1], 1, axis=0))


def _peer_apply_kernel(ex_ref, exn_ref, g_ref, h_ref, ln_g_ref, ln_b_ref, uv_hbm, o_ref, buf, pbuf0, pbuf1, wbuf0, wbuf1, zbuf, sem):
    i = pl.program_id(0)
    n = pl.num_programs(0)
    tk, E = ex_ref.shape
    slot = lax.rem(i, 2)
    nslot = 1 - slot
    pbuf, wbuf = (pbuf0, pbuf1), (wbuf0, wbuf1)

    def row_copy(e, s, t, j):
        return pltpu.make_async_copy(uv_hbm.at[e], buf.at[s * tk + t, j], sem.at[s, t])

    def start_rows(idx_ref, s, t, j0, j1):
        for j in range(j0, j1):
            row_copy(idx_ref[t, j], s, t, j).start(priority=j % 2)

    def start_token(idx_ref, s, t):
        start_rows(idx_ref, s, t, 0, E)

    def wait_token(s, t):
        pltpu.make_async_copy(buf.at[s * tk + t], buf.at[s * tk + t], sem.at[s, t]).wait()

    @pl.when(i == 0)
    def _():
        def body(t, carry):
            start_token(ex_ref, slot, t)
            return carry
        lax.fori_loop(0, tk, body, 0)

    sub = lax.broadcasted_iota(jnp.int32, (SUBLANES, LANES), 0)
    lo4, lo2, lo1 = sub < 4, (sub & 2) == 0, (sub & 1) == 0
    eye = lax.broadcasted_iota(jnp.int32, (E, LANES), 0) == lax.broadcasted_iota(jnp.int32, (E, LANES), 1)
    n_acc = 4

    groups = E // SUBLANES
    per_chunk = E // groups

    def dots_group(t, g):
        h = h_ref[t]
        base = slot * tk + t
        return _sum_over_sublanes([buf[base, g * SUBLANES + k, 0:SUBLANES, :] * h for k in range(SUBLANES)],
                                  lo4, lo2, lo1)

    def weights(t, par):
        act = jnp.sum(pbuf[par][...], axis=1, keepdims=True)
        gate = jnp.sum(jnp.where(eye, g_ref[pl.ds(t, 1), :], 0.0), axis=1, keepdims=True)
        return jnp.broadcast_to(gate * _gelu(act), (E, LANES))

    def mix_group(t, par, g, acc):
        base = slot * tk + t
        for e in range(g * SUBLANES, (g + 1) * SUBLANES):
            w = jnp.broadcast_to(wbuf[par][e:e + 1, :], (SUBLANES, LANES))
            term = w * buf[base, e, SUBLANES:2 * SUBLANES, :]
            acc[e % n_acc] = term if acc[e % n_acc] is None else acc[e % n_acc] + term

    def stage(t, par, with_dots, with_weights, with_mix=True):
        if with_dots:
            wait_token(slot, t + 2)
        w_next = weights(t + 1, 1 - par) if with_weights else None
        acc, pending = [None] * n_acc, None
        for g in range(groups):
            if with_mix:
                start_rows(exn_ref, nslot, t, g * per_chunk, (g + 1) * per_chunk)
            if with_dots:
                p = dots_group(t + 2, g)
                if pending is not None:
                    pbuf[par][(g - 1) * SUBLANES:g * SUBLANES, :] = pending
                pending = p
            if with_mix:
                mix_group(t, par, g, acc)
        if with_dots:
            pbuf[par][(groups - 1) * SUBLANES:, :] = pending
        if with_mix:
            zbuf[t] = ALPHA * h_ref[t] + ((acc[0] + acc[1]) + (acc[2] + acc[3]))
        if with_weights:
            wbuf[1 - par][...] = w_next

    stage(-2, 0, True, False, with_mix=False)
    stage(-1, 1, True, True, with_mix=False)

    def token_pair(tt, carry):
        t = 2 * tt
        stage(t, 0, True, True)
        stage(t + 1, 1, True, True)
        return carry

    assert tk % 2 == 0 and tk >= 4
    lax.fori_loop(0, (tk - 2) // 2, token_pair, 0)

    stage(tk - 2, 0, False, True)
    stage(tk - 1, 1, False, False)


    z = zbuf[...]
    total = lambda a: jnp.sum(jnp.sum(a, axis=2, keepdims=True), axis=1, keepdims=True)
    inv_d = 1.0 / (SUBLANES * LANES)
    zc = z - total(z) * inv_d
    var = total(zc * zc) * inv_d
    o_ref[...] = zc * lax.rsqrt(var + EPS) * ln_g_ref[...] + ln_b_ref[...]

    @pl.when(i == n - 1)
    def _():
        def body(t, carry):
            wait_token(nslot, t)
            return carry
        lax.fori_loop(0, tk, body, 0)


def _peer_apply(h1, experts, gates, peer_u, peer_v, ln2_g, ln2_b, tk=16):
    T, D = h1.shape
    E = experts.shape[1]
    assert D == SUBLANES * LANES and E == LANES
    n = T // tk
    tile = lambda a: a.reshape(-1, SUBLANES, LANES)
    uv = jnp.concatenate([tile(peer_u), tile(peer_v)], axis=1)
    row = lambda i: (i, 0)
    nxt = lambda i: (jnp.minimum(i + 1, n - 1), 0)
    row3 = lambda i: (i, 0, 0)
    fixed = lambda i: (0, 0)
    out = pl.pallas_call(
        _peer_apply_kernel,
        grid=(n,),
        in_specs=[pl.BlockSpec((tk, E), row, memory_space=pltpu.SMEM),
                  pl.BlockSpec((tk, E), nxt, memory_space=pltpu.SMEM),
                  pl.BlockSpec((tk, E), row), pl.BlockSpec((tk, SUBLANES, LANES), row3),
                  pl.BlockSpec((SUBLANES, LANES), fixed), pl.BlockSpec((SUBLANES, LANES), fixed),
                  pl.BlockSpec(memory_space=pl.ANY)],
        out_specs=pl.BlockSpec((tk, SUBLANES, LANES), row3),
        out_shape=jax.ShapeDtypeStruct((T, SUBLANES, LANES), F32),
        scratch_shapes=[pltpu.VMEM((2 * tk, E, 2 * SUBLANES, LANES), F32)]
                       + [pltpu.VMEM((E, LANES), F32)] * 4
                       + [pltpu.VMEM((tk, SUBLANES, LANES), F32), pltpu.SemaphoreType.DMA((2, tk))],
        compiler_params=_params("arbitrary"),
        name="peer_apply",
    )(experts, experts, gates, tile(h1), ln2_g.reshape(SUBLANES, LANES), ln2_b.reshape(SUBLANES, LANES), uv)
    return out.reshape(T, D)


def kernel(x, w_in, b_in, attn_sinks, conv_w, conv_b, gate_a_w, gate_a_b, gate_x_w, gate_x_b, lru_lambda, norm_attn_g, norm_rnn_g, w_out, b_out, ln1_g, ln1_b, peer_w_q, peer_keys_1, peer_keys_2, peer_u, peer_v, ln2_g, ln2_b):
    B, S, D = x.shape
    x2 = x.reshape(B * S, D)
    rnn_width = conv_w.shape[1]
    q, k, v, xr, gate_in = _in_proj(x2, w_in, b_in, rnn_width)
    mixed_attn = _attention(q, k, v, attn_sinks, norm_attn_g, B, S).reshape(B * S, ATTN_WIDTH)
    mixed_rnn = _rglru(xr, gate_in, conv_w, conv_b, gate_a_w, gate_a_b, gate_x_w, gate_x_b, lru_lambda,
                       norm_rnn_g, B, S).reshape(B * S, rnn_width)
    h1, scores_t = _post_mix(x2, mixed_attn, mixed_rnn, w_out, b_out, ln1_g, ln1_b, peer_w_q,
                             peer_keys_1, peer_keys_2)
    experts, gates = _peer_topk(scores_t)
    out = _peer_apply(h1, experts, gates, peer_u, peer_v, ln2_g, ln2_b)
    return out.reshape(B, S, D)
```

```python
import functools
import math

import jax
import jax.numpy as jnp
from jax import lax
from jax.experimental import pallas as pl
from jax.experimental.pallas import tpu as pltpu

F32 = jnp.float32
BF16 = jnp.bfloat16

ATTN_HEADS = 8
KV_HEADS = 2
HEAD_DIM = 64
GQA_GROUP = ATTN_HEADS // KV_HEADS
ATTN_WIDTH = ATTN_HEADS * HEAD_DIM
KV_WIDTH = KV_HEADS * HEAD_DIM
WINDOW = 128
ATTN_BLOCK = 128
ROPE_DIM = HEAD_DIM // 4
ROPE_THETA = 500000.0
RNN_HEADS = 8
CONV_WIDTH = 4
LRU_C = 8.0
PEER_HEADS = 8
PEER_N_KEYS = 128
PEER_HALF_DIM = 128
PEER_TOPK = 16
DEPTH = 1
ALPHA = (2.0 * DEPTH) ** 0.25
EPS = 1e-5

LANES = 128
SUBLANES = 8
VMEM_LIMIT_BYTES = 56 * 1024 * 1024


def _params(*semantics):
    return pltpu.CompilerParams(dimension_semantics=semantics, vmem_limit_bytes=VMEM_LIMIT_BYTES)


def _gelu(x):
    c = math.sqrt(2.0 / math.pi)
    return 0.5 * x * (1.0 + jnp.tanh(c * (x + 0.044715 * (x * x * x))))


def _layer_norm(x, g, b):
    mu = jnp.mean(x, axis=-1, keepdims=True)
    xc = x - mu
    var = jnp.mean(xc * xc, axis=-1, keepdims=True)
    return xc * lax.rsqrt(var + EPS) * g + b


def _rms_norm(x, g):
    return x * lax.rsqrt(jnp.mean(x * x, axis=-1, keepdims=True) + EPS) * g


def _in_proj_kernel(x_ref, w_ref, b_ref, q_ref, k_ref, v_ref, xr_ref, g_ref):
    z = jnp.dot(x_ref[...].astype(BF16), w_ref[...], preferred_element_type=F32) + b_ref[...]
    c0 = q_ref.shape[1]
    c1 = c0 + k_ref.shape[1]
    c2 = c1 + v_ref.shape[1]
    c3 = c2 + xr_ref.shape[1]
    q_ref[...] = z[:, :c0]
    k_ref[...] = z[:, c0:c1]
    v_ref[...] = z[:, c1:c2]
    xr_ref[...] = z[:, c2:c3]
    g_ref[...] = z[:, c3:]


def _in_proj(x2, w_in, b_in, rnn_width, tm=512):
    T, D = x2.shape
    n_in = w_in.shape[1]
    widths = (ATTN_WIDTH, KV_WIDTH, KV_WIDTH, rnn_width, rnn_width)
    row = lambda i: (i, 0)
    fixed = lambda i: (0, 0)
    return pl.pallas_call(
        _in_proj_kernel,
        grid=(T // tm,),
        in_specs=[pl.BlockSpec((tm, D), row), pl.BlockSpec((D, n_in), fixed), pl.BlockSpec((1, n_in), fixed)],
        out_specs=[pl.BlockSpec((tm, w), row) for w in widths],
        out_shape=[jax.ShapeDtypeStruct((T, w), F32) for w in widths],
        compiler_params=_params("parallel"),
        name="in_proj",
    )(x2, w_in.astype(BF16), b_in.reshape(1, n_in))


def _rotary(t, cos_t, sin_lo, sin_hi):
    half = ROPE_DIM // 2
    cols = []
    for c in range(t.shape[1] // LANES):
        tc = t[:, c * LANES:(c + 1) * LANES]
        cols.append(tc * cos_t + pltpu.roll(tc, LANES - half, axis=1) * sin_lo + pltpu.roll(tc, half, axis=1) * sin_hi)
    return cols[0] if len(cols) == 1 else jnp.concatenate(cols, axis=1)


def _attention_kernel(sink_ref, q_ref, kc_ref, kp_ref, vc_ref, vp_ref, cos_ref, slo_ref, shi_ref,
                      cosp_ref, slop_ref, ship_ref, g_ref, o_ref):
    n = pl.program_id(1)
    blk = ATTN_BLOCK
    q = _rotary(q_ref[0], cos_ref[...], slo_ref[...], shi_ref[...]) * (HEAD_DIM ** -0.5)
    kc = _rotary(kc_ref[0], cos_ref[...], slo_ref[...], shi_ref[...])
    kp = _rotary(kp_ref[0], cosp_ref[...], slop_ref[...], ship_ref[...])
    kw = jnp.concatenate([kp, kc], axis=0).astype(BF16)
    vw = jnp.concatenate([vp_ref[0], vc_ref[0]], axis=0).astype(BF16)
    qb = q.astype(BF16)

    qi = lax.broadcasted_iota(jnp.int32, (blk, 2 * blk), 0)
    ki = lax.broadcasted_iota(jnp.int32, (blk, 2 * blk), 1)
    diff = blk + qi - ki
    first_key = jnp.where(n == 0, blk, 0)
    valid = (diff >= 0) & (diff < WINDOW) & (ki >= first_key)

    outs = []
    for h in range(ATTN_HEADS):
        kv = h // GQA_GROUP
        qh = qb[:, h * HEAD_DIM:(h + 1) * HEAD_DIM]
        kh = kw[:, kv * HEAD_DIM:(kv + 1) * HEAD_DIM]
        vh = vw[:, kv * HEAD_DIM:(kv + 1) * HEAD_DIM]
        s = lax.dot_general(qh, kh, (((1,), (1,)), ((), ())), preferred_element_type=F32)
        s = jnp.where(valid, s, -jnp.inf)
        sink = sink_ref[h]
        m = jnp.maximum(jnp.max(s, axis=-1, keepdims=True), sink)
        p = jnp.exp(s - m)
        denom = jnp.sum(p, axis=-1, keepdims=True) + jnp.exp(sink - m)
        probs = (p / denom).astype(BF16)
        outs.append(jnp.dot(probs, vh, preferred_element_type=F32))
    o = jnp.concatenate(outs, axis=1)
    o_ref[0] = _rms_norm(o, g_ref[...])


def _rope_tables(S):
    half = ROPE_DIM // 2
    pos = jnp.arange(S, dtype=F32)
    inv_freq = ROPE_THETA ** (-jnp.arange(0, ROPE_DIM, 2, dtype=F32) / ROPE_DIM)
    ang = pos[:, None] * inv_freq[None, :]
    cos, sin = jnp.cos(ang), jnp.sin(ang)
    ones = jnp.ones((S, HEAD_DIM - ROPE_DIM), F32)
    zeros = jnp.zeros((S, HEAD_DIM - ROPE_DIM), F32)
    zh = jnp.zeros((S, half), F32)
    per_head = lambda *parts: jnp.tile(jnp.concatenate(parts, axis=1), (1, LANES // HEAD_DIM))
    return per_head(cos, cos, ones), per_head(-sin, zh, zeros), per_head(zh, sin, zeros)


def _attention(q, k, v, sinks, norm_g, B, S):
    blk = ATTN_BLOCK
    nb = S // blk
    cos_t, sin_lo, sin_hi = _rope_tables(S)
    cur = lambda b, n: (b, n, 0)
    prev = lambda b, n: (b, jnp.maximum(n - 1, 0), 0)
    tcur = lambda b, n: (n, 0)
    tprev = lambda b, n: (jnp.maximum(n - 1, 0), 0)
    tab = lambda im: pl.BlockSpec((blk, LANES), im)
    return pl.pallas_call(
        _attention_kernel,
        grid=(B, nb),
        in_specs=[pl.BlockSpec(memory_space=pltpu.SMEM),
                  pl.BlockSpec((1, blk, ATTN_WIDTH), cur),
                  pl.BlockSpec((1, blk, KV_WIDTH), cur), pl.BlockSpec((1, blk, KV_WIDTH), prev),
                  pl.BlockSpec((1, blk, KV_WIDTH), cur), pl.BlockSpec((1, blk, KV_WIDTH), prev),
                  tab(tcur), tab(tcur), tab(tcur), tab(tprev), tab(tprev), tab(tprev),
                  pl.BlockSpec((1, ATTN_WIDTH), lambda b, n: (0, 0))],
        out_specs=pl.BlockSpec((1, blk, ATTN_WIDTH), cur),
        out_shape=jax.ShapeDtypeStruct((B, S, ATTN_WIDTH), F32),
        compiler_params=_params("parallel", "parallel"),
        name="attention",
    )(sinks, q.reshape(B, S, ATTN_WIDTH), k.reshape(B, S, KV_WIDTH), k.reshape(B, S, KV_WIDTH),
      v.reshape(B, S, KV_WIDTH), v.reshape(B, S, KV_WIDTH),
      cos_t, sin_lo, sin_hi, cos_t, sin_lo, sin_hi, norm_g.reshape(1, ATTN_WIDTH))


def _rglru_kernel(xr_ref, gin_ref, cw_ref, cb_ref, wg_ref, bg_ref, lam_ref, g_ref, o_ref, tail_ref, h_ref):
    ts, C = xr_ref.shape[1], xr_ref.shape[2]

    @pl.when(pl.program_id(1) == 0)
    def _():
        tail_ref[...] = jnp.zeros_like(tail_ref)
        h_ref[...] = jnp.zeros_like(h_ref)

    xr = xr_ref[0]
    xcat = jnp.concatenate([tail_ref[...], xr], axis=0)
    tail_ref[...] = xr[ts - SUBLANES:, :]
    xc = cb_ref[...] + cw_ref[CONV_WIDTH - 1:CONV_WIDTH, :] * xr
    for j in range(CONV_WIDTH - 1):
        back = CONV_WIDTH - 1 - j
        xc = xc + cw_ref[j:j + 1, :] * pltpu.roll(xcat, back, axis=0)[SUBLANES:, :]

    gates = jnp.dot(xc.astype(BF16), wg_ref[...], preferred_element_type=F32) + bg_ref[...]
    r = jax.nn.sigmoid(gates[:, :C])
    i = jax.nn.sigmoid(gates[:, C:])
    lam = lam_ref[...]
    softplus_neg_lam = jnp.maximum(-lam, 0.0) + jnp.log1p(jnp.exp(-jnp.abs(lam)))
    log_a = -LRU_C * r * softplus_neg_lam
    a = jnp.exp(log_a)
    b = jnp.sqrt((1.0 + jnp.exp(2.0 * log_a)) * jnp.tanh(-log_a)) * (i * xc)

    row = lax.broadcasted_iota(jnp.int32, (ts, C), 0)
    d = 1
    while d < ts:
        keep = row >= d
        b = jnp.where(keep, a * pltpu.roll(b, d, axis=0) + b, b)
        a = jnp.where(keep, a * pltpu.roll(a, d, axis=0), a)
        d *= 2
    h = b + a * h_ref[0:1, :]
    h_ref[...] = jnp.broadcast_to(h[ts - 1:ts, :], h_ref.shape)
    o_ref[0] = _rms_norm(_gelu(gin_ref[0]) * h, g_ref[...])


def _block_diag(w):
    H, a, b = w.shape
    eye = jnp.eye(H, dtype=w.dtype)
    return (w[:, :, None, :] * eye[:, None, :, None]).reshape(H * a, H * b)


def _rglru(xr, gate_in, conv_w, conv_b, gate_a_w, gate_a_b, gate_x_w, gate_x_b, lru_lambda, norm_g, B, S, ts=256):
    C = xr.shape[-1]
    wg = jnp.concatenate([_block_diag(gate_a_w), _block_diag(gate_x_w)], axis=1).astype(BF16)
    bg = jnp.concatenate([gate_a_b, gate_x_b]).reshape(1, 2 * C)
    cur = lambda b, n: (b, n, 0)
    fixed = lambda b, n: (0, 0)
    return pl.pallas_call(
        _rglru_kernel,
        grid=(B, S // ts),
        in_specs=[pl.BlockSpec((1, ts, C), cur), pl.BlockSpec((1, ts, C), cur),
                  pl.BlockSpec((CONV_WIDTH, C), fixed), pl.BlockSpec((1, C), fixed),
                  pl.BlockSpec((C, 2 * C), fixed), pl.BlockSpec((1, 2 * C), fixed),
                  pl.BlockSpec((1, C), fixed), pl.BlockSpec((1, C), fixed)],
        out_specs=pl.BlockSpec((1, ts, C), cur),
        out_shape=jax.ShapeDtypeStruct((B, S, C), F32),
        scratch_shapes=[pltpu.VMEM((SUBLANES, C), F32), pltpu.VMEM((SUBLANES, C), F32)],
        compiler_params=_params("parallel", "arbitrary"),
        name="rglru",
    )(xr.reshape(B, S, C), gate_in.reshape(B, S, C), conv_w, conv_b.reshape(1, C), wg, bg,
      lru_lambda.reshape(1, C), norm_g.reshape(1, C))


def _post_mix_kernel(x_ref, ma_ref, mr_ref, woa_ref, wor_ref, bo_ref, g1_ref, b1_ref, wq_ref, k1_ref, k2_ref,
                     h_ref, s_ref):
    y = (jnp.dot(ma_ref[...].astype(BF16), woa_ref[...], preferred_element_type=F32)
         + jnp.dot(mr_ref[...].astype(BF16), wor_ref[...], preferred_element_type=F32) + bo_ref[...])
    h = _layer_norm(ALPHA * x_ref[...] + y, g1_ref[...], b1_ref[...])
    h_ref[...] = h
    q = jnp.dot(h.astype(BF16), wq_ref[...], preferred_element_type=F32).astype(BF16)
    nt = (((1,), (1,)), ((), ()))
    for hd in range(PEER_HEADS):
        for half, k_ref in enumerate((k1_ref, k2_ref)):
            c = (hd * 2 + half) * PEER_HALF_DIM
            for blk in range(s_ref.shape[1]):
                qb = q[blk * LANES:(blk + 1) * LANES, c:c + PEER_HALF_DIM]
                s_ref[hd * 2 + half, blk] = lax.dot_general(k_ref[...], qb, nt, preferred_element_type=F32)


def _post_mix(x2, mixed_attn, mixed_rnn, w_out, b_out, ln1_g, ln1_b, peer_w_q, keys_1, keys_2, tm=256):
    T, D = x2.shape
    wa, wr = mixed_attn.shape[1], mixed_rnn.shape[1]
    nq = peer_w_q.shape[1]
    row = lambda i: (i, 0)
    fixed = lambda i: (0, 0)
    full = lambda a: pl.BlockSpec(a.shape, fixed)
    wo = w_out.astype(BF16)
    woa, wor = wo[:wa], wo[wa:]
    wq = peer_w_q.astype(BF16)
    k1, k2 = keys_1.astype(BF16), keys_2.astype(BF16)
    vec = lambda a: a.reshape(1, -1)
    return pl.pallas_call(
        _post_mix_kernel,
        grid=(T // tm,),
        in_specs=[pl.BlockSpec((tm, D), row), pl.BlockSpec((tm, wa), row), pl.BlockSpec((tm, wr), row),
                  full(woa), full(wor), pl.BlockSpec((1, D), fixed), pl.BlockSpec((1, D), fixed),
                  pl.BlockSpec((1, D), fixed), full(wq), full(k1), full(k2)],
        out_specs=[pl.BlockSpec((tm, D), row),
                   pl.BlockSpec((2 * PEER_HEADS, tm // LANES, PEER_N_KEYS, LANES), lambda i: (0, i, 0, 0))],
        out_shape=[jax.ShapeDtypeStruct((T, D), F32),
                   jax.ShapeDtypeStruct((2 * PEER_HEADS, T // LANES, PEER_N_KEYS, LANES), F32)],
        compiler_params=_params("parallel"),
        name="post_mix",
    )(x2, mixed_attn, mixed_rnn, woa, wor, vec(b_out), vec(ln1_g), vec(ln1_b), wq, k1, k2)


def _before(a, b):
    return (a[0] > b[0]) | ((a[0] == b[0]) & (a[1] < b[1]))


def _pick(cond, a, b):
    return tuple(jnp.where(cond, x, y) for x, y in zip(a, b))


def _compare_exchange(xs, i, j):
    swap = _before(xs[j], xs[i])
    xs[i], xs[j] = _pick(swap, xs[j], xs[i]), _pick(swap, xs[i], xs[j])


def _batcher_pairs(n):
    pairs, p = [], 1
    while p < n:
        k = p
        while k >= 1:
            for j in range(k % p, n - k, 2 * k):
                for i in range(min(k, n - j - k)):
                    if (i + j) // (2 * p) == (i + j + k) // (2 * p):
                        pairs.append((i + j, i + j + k))
            k //= 2
        p *= 2
    return pairs


def _sort(xs):
    xs = list(xs)
    for i, j in _batcher_pairs(len(xs)):
        _compare_exchange(xs, i, j)
    return xs


def _best_of_two_sorted(a, b, sort_result=True):
    n = len(a)
    c = [_pick(_before(b[n - 1 - i], a[i]), b[n - 1 - i], a[i]) for i in range(n)]
    if sort_result:
        d = n // 2
        while d >= 1:
            for i in range(n):
                if not i & d:
                    _compare_exchange(c, i, i + d)
            d //= 2
    return c


def _peer_topk_kernel(s_ref, e_ref, g_ref, km, sv, si):
    K = PEER_TOPK
    n_groups = PEER_N_KEYS // K
    tile = lambda fill: jnp.full((SUBLANES, LANES), fill, jnp.int32)

    def head(hd, carry):
        for half in range(2):
            for key in range(PEER_N_KEYS):
                km[key] = s_ref[2 * hd + half, :, key, :]

            def sort_group(g, c, half=half):
                xs = _sort([(km[g * K + r], tile(g * K + r)) for r in range(K)])
                for r in range(K):
                    sv[half, g * K + r], si[half, g * K + r] = xs[r]
                return c

            def merge(m, c, half=half):
                read = lambda slot: [(sv[half, slot * K + r], si[half, slot * K + r]) for r in range(K)]
                xs = _best_of_two_sorted(read(2 * m), read(2 * m + 1))
                for r in range(K):
                    sv[half, (n_groups + m) * K + r], si[half, (n_groups + m) * K + r] = xs[r]
                return c

            lax.fori_loop(0, n_groups, sort_group, 0)
            lax.fori_loop(0, n_groups - 1, merge, 0)

        last = (2 * n_groups - 2) * K
        v1 = [(sv[0, last + r], si[0, last + r]) for r in range(K)]
        v2 = [(sv[1, last + r], si[1, last + r]) for r in range(K)]
        def cand(i, j):
            return (v1[i][0] + v2[j][0], tile(i * K + j), v1[i][1] * PEER_N_KEYS + v2[j][1])
        rows = [[cand(i, j) for j in range(K // (i + 1))] for i in range(K)]
        assert K == 16
        g1 = _sort(rows[1] + rows[2] + rows[4])
        g2 = _sort(rows[3] + rows[5] + rows[6] + rows[7] + [r[0] for r in rows[8:14]])
        top = _best_of_two_sorted(_best_of_two_sorted(rows[0], g1), g2)
        top[K - 1] = _pick(_before(rows[14][0], top[K - 1]), rows[14][0], top[K - 1])
        top[K - 2] = _pick(_before(rows[15][0], top[K - 2]), rows[15][0], top[K - 2])
        m = top[0][0]
        for t in top[1:]:
            m = jnp.maximum(m, t[0])
        p = [jnp.exp(t[0] - m) for t in top]
        denom = p[0]
        for x in p[1:]:
            denom = denom + x
        for r in range(K):
            e_ref[hd * K + r] = top[r][2]
            g_ref[hd * K + r] = p[r] / denom
        return carry

    lax.fori_loop(0, PEER_HEADS, head, 0)


def _peer_topk(scores):
    n_sets, n_blocks, n_keys, _ = scores.shape
    T = n_blocks * LANES
    E = PEER_HEADS * PEER_TOPK
    out_map = lambda i: (0, i, 0)
    e3, g3 = pl.pallas_call(
        _peer_topk_kernel,
        grid=(n_blocks // SUBLANES,),
        in_specs=[pl.BlockSpec((n_sets, SUBLANES, n_keys, LANES), lambda i: (0, i, 0, 0))],
        out_specs=[pl.BlockSpec((E, SUBLANES, LANES), out_map), pl.BlockSpec((E, SUBLANES, LANES), out_map)],
        out_shape=[jax.ShapeDtypeStruct((E, n_blocks, LANES), jnp.int32),
                   jax.ShapeDtypeStruct((E, n_blocks, LANES), F32)],
        scratch_shapes=[pltpu.VMEM((n_keys, SUBLANES, LANES), F32),
                        pltpu.VMEM((2, (2 * n_keys // PEER_TOPK - 1) * PEER_TOPK, SUBLANES, LANES), F32),
                        pltpu.VMEM((2, (2 * n_keys // PEER_TOPK - 1) * PEER_TOPK, SUBLANES, LANES), jnp.int32)],
        compiler_params=_params("parallel"),
        name="peer_topk",
    )(scores)
    return e3.reshape(E, T).T, g3.reshape(E, T).T


def _sum_over_sublanes(p, lo4, lo2, lo1):
    z = [jnp.where(lo4, p[k], p[k + 4]) + pltpu.roll(jnp.where(lo4, p[k + 4], p[k]), 4, axis=0) for k in range(4)]
    y = [jnp.where(lo2, z[k] + pltpu.roll(z[k], SUBLANES - 2, axis=0), z[k + 2] + pltpu.roll(z[k + 2], 2, axis=0))
         for k in range(2)]
    return jnp.where(lo1, y[0] + pltpu.roll(y[0], SUBLANES - 1, axis=0), y[1] + pltpu.roll(y[1], 1, axis=0))


def _peer_apply_kernel(ex_ref, exn_ref, g_ref, h_ref, ln_g_ref, ln_b_ref, uv_hbm, o_ref, buf, pbuf0, pbuf1, wbuf0, wbuf1, zbuf, sem):
    i = pl.program_id(0)
    n = pl.num_programs(0)
    tk, E = ex_ref.shape
    slot = lax.rem(i, 2)
    nslot = 1 - slot
    pbuf, wbuf = (pbuf0, pbuf1), (wbuf0, wbuf1)

    def row_copy(e, s, t, j):
        return pltpu.make_async_copy(uv_hbm.at[e], buf.at[s * tk + t, j], sem.at[s, t])

    def start_rows(idx_ref, s, t, j0, j1):
        for j in range(j0, j1):
            row_copy(idx_ref[t, j], s, t, j).start(priority=j % 2)

    def start_token(idx_ref, s, t):
        start_rows(idx_ref, s, t, 0, E)

    def wait_token(s, t):
        pltpu.make_async_copy(buf.at[s * tk + t], buf.at[s * tk + t], sem.at[s, t]).wait()

    @pl.when(i == 0)
    def _():
        def body(t, carry):
            start_token(ex_ref, slot, t)
            return carry
        lax.fori_loop(0, tk, body, 0)

    sub = lax.broadcasted_iota(jnp.int32, (SUBLANES, LANES), 0)
    lo4, lo2, lo1 = sub < 4, (sub & 2) == 0, (sub & 1) == 0
    eye = lax.broadcasted_iota(jnp.int32, (E, LANES), 0) == lax.broadcasted_iota(jnp.int32, (E, LANES), 1)
    n_acc = 4

    groups = E // SUBLANES
    per_chunk = E // groups

    def dots_group(t, g):
        h = h_ref[t]
        base = slot * tk + t
        return _sum_over_sublanes([buf[base, g * SUBLANES + k, 0:SUBLANES, :] * h for k in range(SUBLANES)],
                                  lo4, lo2, lo1)

    def weights(t, par):
        act = jnp.sum(pbuf[par][...], axis=1, keepdims=True)
        gate = jnp.sum(jnp.where(eye, g_ref[pl.ds(t, 1), :], 0.0), axis=1, keepdims=True)
        return jnp.broadcast_to(gate * _gelu(act), (E, LANES))

    def mix_group(t, par, g, acc):
        base = slot * tk + t
        for e in range(g * SUBLANES, (g + 1) * SUBLANES):
            w = jnp.broadcast_to(wbuf[par][e:e + 1, :], (SUBLANES, LANES))
            term = w * buf[base, e, SUBLANES:2 * SUBLANES, :]
            acc[e % n_acc] = term if acc[e % n_acc] is None else acc[e % n_acc] + term

    def stage(t, par, with_dots, with_weights, with_mix=True):
        if with_dots:
            wait_token(slot, t + 2)
        w_next = weights(t + 1, 1 - par) if with_weights else None
        acc, pending = [None] * n_acc, None
        for g in range(groups):
            if with_mix:
                start_rows(exn_ref, nslot, t, g * per_chunk, (g + 1) * per_chunk)
            if with_dots:
                p = dots_group(t + 2, g)
                if pending is not None:
                    pbuf[par][(g - 1) * SUBLANES:g * SUBLANES, :] = pending
                pending = p
            if with_mix:
                mix_group(t, par, g, acc)
        if with_dots:
            pbuf[par][(groups - 1) * SUBLANES:, :] = pending
        if with_mix:
            zbuf[t] = ALPHA * h_ref[t] + ((acc[0] + acc[1]) + (acc[2] + acc[3]))
        if with_weights:
            wbuf[1 - par][...] = w_next

    stage(-2, 0, True, False, with_mix=False)
    stage(-1, 1, True, True, with_mix=False)

    def token_pair(tt, carry):
        t = 2 * tt
        stage(t, 0, True, True)
        stage(t + 1, 1, True, True)
        return carry

    assert tk % 2 == 0 and tk >= 4
    lax.fori_loop(0, (tk - 2) // 2, token_pair, 0)

    stage(tk - 2, 0, False, True)
    stage(tk - 1, 1, False, False)


    z = zbuf[...]
    total = lambda a: jnp.sum(jnp.sum(a, axis=2, keepdims=True), axis=1, keepdims=True)
    inv_d = 1.0 / (SUBLANES * LANES)
    zc = z - total(z) * inv_d
    var = total(zc * zc) * inv_d
    o_ref[...] = zc * lax.rsqrt(var + EPS) * ln_g_ref[...] + ln_b_ref[...]

    @pl.when(i == n - 1)
    def _():
        def body(t, carry):
            wait_token(nslot, t)
            return carry
        lax.fori_loop(0, tk, body, 0)


def _peer_apply(h1, experts, gates, peer_u, peer_v, ln2_g, ln2_b, tk=16):
    T, D = h1.shape
    E = experts.shape[1]
    assert D == SUBLANES * LANES and E == LANES
    n = T // tk
    tile = lambda a: a.reshape(-1, SUBLANES, LANES)
    uv = jnp.concatenate([tile(peer_u), tile(peer_v)], axis=1)
    row = lambda i: (i, 0)
    nxt = lambda i: (jnp.minimum(i + 1, n - 1), 0)
    row3 = lambda i: (i, 0, 0)
    fixed = lambda i: (0, 0)
    out = pl.pallas_call(
        _peer_apply_kernel,
        grid=(n,),
        in_specs=[pl.BlockSpec((tk, E), row, memory_space=pltpu.SMEM),
                  pl.BlockSpec((tk, E), nxt, memory_space=pltpu.SMEM),
                  pl.BlockSpec((tk, E), row), pl.BlockSpec((tk, SUBLANES, LANES), row3),
                  pl.BlockSpec((SUBLANES, LANES), fixed), pl.BlockSpec((SUBLANES, LANES), fixed),
                  pl.BlockSpec(memory_space=pl.ANY)],
        out_specs=pl.BlockSpec((tk, SUBLANES, LANES), row3),
        out_shape=jax.ShapeDtypeStruct((T, SUBLANES, LANES), F32),
        scratch_shapes=[pltpu.VMEM((2 * tk, E, 2 * SUBLANES, LANES), F32)]
                       + [pltpu.VMEM((E, LANES), F32)] * 4
                       + [pltpu.VMEM((tk, SUBLANES, LANES), F32), pltpu.SemaphoreType.DMA((2, tk))],
        compiler_params=_params("arbitrary"),
        name="peer_apply",
    )(experts, experts, gates, tile(h1), ln2_g.reshape(SUBLANES, LANES), ln2_b.reshape(SUBLANES, LANES), uv)
    return out.reshape(T, D)


def kernel(x, w_in, b_in, attn_sinks, conv_w, conv_b, gate_a_w, gate_a_b, gate_x_w, gate_x_b, lru_lambda, norm_attn_g, norm_rnn_g, w_out, b_out, ln1_g, ln1_b, peer_w_q, peer_keys_1, peer_keys_2, peer_u, peer_v, ln2_g, ln2_b):
    B, S, D = x.shape
    x2 = x.reshape(B * S, D)
    rnn_width = conv_w.shape[1]
    q, k, v, xr, gate_in = _in_proj(x2, w_in, b_in, rnn_width)
    mixed_attn = _attention(q, k, v, attn_sinks, norm_attn_g, B, S).reshape(B * S, ATTN_WIDTH)
    mixed_rnn = _rglru(xr, gate_in, conv_w, conv_b, gate_a_w, gate_a_b, gate_x_w, gate_x_b, lru_lambda,
                       norm_rnn_g, B, S).reshape(B * S, rnn_width)
    h1, scores_t = _post_mix(x2, mixed_attn, mixed_rnn, w_out, b_out, ln1_g, ln1_b, peer_w_q,
                             peer_keys_1, peer_keys_2)
    experts, gates = _peer_topk(scores_t)
    out = _peer_apply(h1, experts, gates, peer_u, peer_v, ln2_g, ln2_b)
    return out.reshape(B, S, D)
```

```python
import functools
import math

import jax
import jax.numpy as jnp
from jax import lax
from jax.experimental import pallas as pl
from jax.experimental.pallas import tpu as pltpu

F32 = jnp.float32
BF16 = jnp.bfloat16

ATTN_HEADS = 8
KV_HEADS = 2
HEAD_DIM = 64
GQA_GROUP = ATTN_HEADS // KV_HEADS
ATTN_WIDTH = ATTN_HEADS * HEAD_DIM
KV_WIDTH = KV_HEADS * HEAD_DIM
WINDOW = 128
ATTN_BLOCK = 128
ROPE_DIM = HEAD_DIM // 4
ROPE_THETA = 500000.0
RNN_HEADS = 8
CONV_WIDTH = 4
LRU_C = 8.0
PEER_HEADS = 8
PEER_N_KEYS = 128
PEER_HALF_DIM = 128
PEER_TOPK = 16
DEPTH = 1
ALPHA = (2.0 * DEPTH) ** 0.25
EPS = 1e-5

LANES = 128
SUBLANES = 8
VMEM_LIMIT_BYTES = 56 * 1024 * 1024


def _params(*semantics):
    return pltpu.CompilerParams(dimension_semantics=semantics, vmem_limit_bytes=VMEM_LIMIT_BYTES)


def _gelu(x):
    c = math.sqrt(2.0 / math.pi)
    return 0.5 * x * (1.0 + jnp.tanh(c * (x + 0.044715 * (x * x * x))))


def _layer_norm(x, g, b):
    mu = jnp.mean(x, axis=-1, keepdims=True)
    xc = x - mu
    var = jnp.mean(xc * xc, axis=-1, keepdims=True)
    return xc * lax.rsqrt(var + EPS) * g + b


def _rms_norm(x, g):
    return x * lax.rsqrt(jnp.mean(x * x, axis=-1, keepdims=True) + EPS) * g


def _in_proj_kernel(x_ref, w_ref, b_ref, q_ref, k_ref, v_ref, xr_ref, g_ref):
    z = jnp.dot(x_ref[...].astype(BF16), w_ref[...], preferred_element_type=F32) + b_ref[...]
    c0 = q_ref.shape[1]
    c1 = c0 + k_ref.shape[1]
    c2 = c1 + v_ref.shape[1]
    c3 = c2 + xr_ref.shape[1]
    q_ref[...] = z[:, :c0]
    k_ref[...] = z[:, c0:c1]
    v_ref[...] = z[:, c1:c2]
    xr_ref[...] = z[:, c2:c3]
    g_ref[...] = z[:, c3:]


def _in_proj(x2, w_in, b_in, rnn_width, tm=512):
    T, D = x2.shape
    n_in = w_in.shape[1]
    widths = (ATTN_WIDTH, KV_WIDTH, KV_WIDTH, rnn_width, rnn_width)
    row = lambda i: (i, 0)
    fixed = lambda i: (0, 0)
    return pl.pallas_call(
        _in_proj_kernel,
        grid=(T // tm,),
        in_specs=[pl.BlockSpec((tm, D), row), pl.BlockSpec((D, n_in), fixed), pl.BlockSpec((1, n_in), fixed)],
        out_specs=[pl.BlockSpec((tm, w), row) for w in widths],
        out_shape=[jax.ShapeDtypeStruct((T, w), F32) for w in widths],
        compiler_params=_params("parallel"),
        name="in_proj",
    )(x2, w_in.astype(BF16), b_in.reshape(1, n_in))


def _rotary(t, cos_t, sin_lo, sin_hi):
    half = ROPE_DIM // 2
    cols = []
    for c in range(t.shape[1] // LANES):
        tc = t[:, c * LANES:(c + 1) * LANES]
        cols.append(tc * cos_t + pltpu.roll(tc, LANES - half, axis=1) * sin_lo + pltpu.roll(tc, half, axis=1) * sin_hi)
    return cols[0] if len(cols) == 1 else jnp.concatenate(cols, axis=1)


def _attention_kernel(sink_ref, q_ref, kc_ref, kp_ref, vc_ref, vp_ref, cos_ref, slo_ref, shi_ref,
                      cosp_ref, slop_ref, ship_ref, g_ref, o_ref):
    n = pl.program_id(1)
    blk = ATTN_BLOCK
    q = _rotary(q_ref[0], cos_ref[...], slo_ref[...], shi_ref[...]) * (HEAD_DIM ** -0.5)
    kc = _rotary(kc_ref[0], cos_ref[...], slo_ref[...], shi_ref[...])
    kp = _rotary(kp_ref[0], cosp_ref[...], slop_ref[...], ship_ref[...])
    kw = jnp.concatenate([kp, kc], axis=0).astype(BF16)
    vw = jnp.concatenate([vp_ref[0], vc_ref[0]], axis=0).astype(BF16)
    qb = q.astype(BF16)

    qi = lax.broadcasted_iota(jnp.int32, (blk, 2 * blk), 0)
    ki = lax.broadcasted_iota(jnp.int32, (blk, 2 * blk), 1)
    diff = blk + qi - ki
    first_key = jnp.where(n == 0, blk, 0)
    valid = (diff >= 0) & (diff < WINDOW) & (ki >= first_key)

    outs = []
    for h in range(ATTN_HEADS):
        kv = h // GQA_GROUP
        qh = qb[:, h * HEAD_DIM:(h + 1) * HEAD_DIM]
        kh = kw[:, kv * HEAD_DIM:(kv + 1) * HEAD_DIM]
        vh = vw[:, kv * HEAD_DIM:(kv + 1) * HEAD_DIM]
        s = lax.dot_general(qh, kh, (((1,), (1,)), ((), ())), preferred_element_type=F32)
        s = jnp.where(valid, s, -jnp.inf)
        sink = sink_ref[h]
        m = jnp.maximum(jnp.max(s, axis=-1, keepdims=True), sink)
        p = jnp.exp(s - m)
        denom = jnp.sum(p, axis=-1, keepdims=True) + jnp.exp(sink - m)
        probs = (p / denom).astype(BF16)
        outs.append(jnp.dot(probs, vh, preferred_element_type=F32))
    o = jnp.concatenate(outs, axis=1)
    o_ref[0] = _rms_norm(o, g_ref[...])


def _rope_tables(S):
    half = ROPE_DIM // 2
    pos = jnp.arange(S, dtype=F32)
    inv_freq = ROPE_THETA ** (-jnp.arange(0, ROPE_DIM, 2, dtype=F32) / ROPE_DIM)
    ang = pos[:, None] * inv_freq[None, :]
    cos, sin = jnp.cos(ang), jnp.sin(ang)
    ones = jnp.ones((S, HEAD_DIM - ROPE_DIM), F32)
    zeros = jnp.zeros((S, HEAD_DIM - ROPE_DIM), F32)
    zh = jnp.zeros((S, half), F32)
    per_head = lambda *parts: jnp.tile(jnp.concatenate(parts, axis=1), (1, LANES // HEAD_DIM))
    return per_head(cos, cos, ones), per_head(-sin, zh, zeros), per_head(zh, sin, zeros)


def _attention(q, k, v, sinks, norm_g, B, S):
    blk = ATTN_BLOCK
    nb = S // blk
    cos_t, sin_lo, sin_hi = _rope_tables(S)
    cur = lambda b, n: (b, n, 0)
    prev = lambda b, n: (b, jnp.maximum(n - 1, 0), 0)
    tcur = lambda b, n: (n, 0)
    tprev = lambda b, n: (jnp.maximum(n - 1, 0), 0)
    tab = lambda im: pl.BlockSpec((blk, LANES), im)
    return pl.pallas_call(
        _attention_kernel,
        grid=(B, nb),
        in_specs=[pl.BlockSpec(memory_space=pltpu.SMEM),
                  pl.BlockSpec((1, blk, ATTN_WIDTH), cur),
                  pl.BlockSpec((1, blk, KV_WIDTH), cur), pl.BlockSpec((1, blk, KV_WIDTH), prev),
                  pl.BlockSpec((1, blk, KV_WIDTH), cur), pl.BlockSpec((1, blk, KV_WIDTH), prev),
                  tab(tcur), tab(tcur), tab(tcur), tab(tprev), tab(tprev), tab(tprev),
                  pl.BlockSpec((1, ATTN_WIDTH), lambda b, n: (0, 0))],
        out_specs=pl.BlockSpec((1, blk, ATTN_WIDTH), cur),
        out_shape=jax.ShapeDtypeStruct((B, S, ATTN_WIDTH), F32),
        compiler_params=_params("parallel", "parallel"),
        name="attention",
    )(sinks, q.reshape(B, S, ATTN_WIDTH), k.reshape(B, S, KV_WIDTH), k.reshape(B, S, KV_WIDTH),
      v.reshape(B, S, KV_WIDTH), v.reshape(B, S, KV_WIDTH),
      cos_t, sin_lo, sin_hi, cos_t, sin_lo, sin_hi, norm_g.reshape(1, ATTN_WIDTH))


def _rglru_kernel(xr_ref, gin_ref, cw_ref, cb_ref, wg_ref, bg_ref, lam_ref, g_ref, o_ref, tail_ref, h_ref):
    ts, C = xr_ref.shape[1], xr_ref.shape[2]

    @pl.when(pl.program_id(1) == 0)
    def _():
        tail_ref[...] = jnp.zeros_like(tail_ref)
        h_ref[...] = jnp.zeros_like(h_ref)

    xr = xr_ref[0]
    xcat = jnp.concatenate([tail_ref[...], xr], axis=0)
    tail_ref[...] = xr[ts - SUBLANES:, :]
    xc = cb_ref[...] + cw_ref[CONV_WIDTH - 1:CONV_WIDTH, :] * xr
    for j in range(CONV_WIDTH - 1):
        back = CONV_WIDTH - 1 - j
        xc = xc + cw_ref[j:j + 1, :] * pltpu.roll(xcat, back, axis=0)[SUBLANES:, :]

    gates = jnp.dot(xc.astype(BF16), wg_ref[...], preferred_element_type=F32) + bg_ref[...]
    r = jax.nn.sigmoid(gates[:, :C])
    i = jax.nn.sigmoid(gates[:, C:])
    lam = lam_ref[...]
    softplus_neg_lam = jnp.maximum(-lam, 0.0) + jnp.log1p(jnp.exp(-jnp.abs(lam)))
    log_a = -LRU_C * r * softplus_neg_lam
    a = jnp.exp(log_a)
    b = jnp.sqrt((1.0 + jnp.exp(2.0 * log_a)) * jnp.tanh(-log_a)) * (i * xc)

    row = lax.broadcasted_iota(jnp.int32, (ts, C), 0)
    d = 1
    while d < ts:
        keep = row >= d
        b = jnp.where(keep, a * pltpu.roll(b, d, axis=0) + b, b)
        a = jnp.where(keep, a * pltpu.roll(a, d, axis=0), a)
        d *= 2
    h = b + a * h_ref[0:1, :]
    h_ref[...] = jnp.broadcast_to(h[ts - 1:ts, :], h_ref.shape)
    o_ref[0] = _rms_norm(_gelu(gin_ref[0]) * h, g_ref[...])


def _block_diag(w):
    H, a, b = w.shape
    eye = jnp.eye(H, dtype=w.dtype)
    return (w[:, :, None, :] * eye[:, None, :, None]).reshape(H * a, H * b)


def _rglru(xr, gate_in, conv_w, conv_b, gate_a_w, gate_a_b, gate_x_w, gate_x_b, lru_lambda, norm_g, B, S, ts=256):
    C = xr.shape[-1]
    wg = jnp.concatenate([_block_diag(gate_a_w), _block_diag(gate_x_w)], axis=1).astype(BF16)
    bg = jnp.concatenate([gate_a_b, gate_x_b]).reshape(1, 2 * C)
    cur = lambda b, n: (b, n, 0)
    fixed = lambda b, n: (0, 0)
    return pl.pallas_call(
        _rglru_kernel,
        grid=(B, S // ts),
        in_specs=[pl.BlockSpec((1, ts, C), cur), pl.BlockSpec((1, ts, C), cur),
                  pl.BlockSpec((CONV_WIDTH, C), fixed), pl.BlockSpec((1, C), fixed),
                  pl.BlockSpec((C, 2 * C), fixed), pl.BlockSpec((1, 2 * C), fixed),
                  pl.BlockSpec((1, C), fixed), pl.BlockSpec((1, C), fixed)],
        out_specs=pl.BlockSpec((1, ts, C), cur),
        out_shape=jax.ShapeDtypeStruct((B, S, C), F32),
        scratch_shapes=[pltpu.VMEM((SUBLANES, C), F32), pltpu.VMEM((SUBLANES, C), F32)],
        compiler_params=_params("parallel", "arbitrary"),
        name="rglru",
    )(xr.reshape(B, S, C), gate_in.reshape(B, S, C), conv_w, conv_b.reshape(1, C), wg, bg,
      lru_lambda.reshape(1, C), norm_g.reshape(1, C))


def _post_mix_kernel(x_ref, ma_ref, mr_ref, woa_ref, wor_ref, bo_ref, g1_ref, b1_ref, wq_ref, k1_ref, k2_ref,
                     h_ref, s_ref):
    y = (jnp.dot(ma_ref[...].astype(BF16), woa_ref[...], preferred_element_type=F32)
         + jnp.dot(mr_ref[...].astype(BF16), wor_ref[...], preferred_element_type=F32) + bo_ref[...])
    h = _layer_norm(ALPHA * x_ref[...] + y, g1_ref[...], b1_ref[...])
    h_ref[...] = h
    q = jnp.dot(h.astype(BF16), wq_ref[...], preferred_element_type=F32).astype(BF16)
    nt = (((1,), (1,)), ((), ()))
    for hd in range(PEER_HEADS):
        for half, k_ref in enumerate((k1_ref, k2_ref)):
            c = (hd * 2 + half) * PEER_HALF_DIM
            for blk in range(s_ref.shape[1]):
                qb = q[blk * LANES:(blk + 1) * LANES, c:c + PEER_HALF_DIM]
                s_ref[hd * 2 + half, blk] = lax.dot_general(k_ref[...], qb, nt, preferred_element_type=F32)


def _post_mix(x2, mixed_attn, mixed_rnn, w_out, b_out, ln1_g, ln1_b, peer_w_q, keys_1, keys_2, tm=256):
    T, D = x2.shape
    wa, wr = mixed_attn.shape[1], mixed_rnn.shape[1]
    nq = peer_w_q.shape[1]
    row = lambda i: (i, 0)
    fixed = lambda i: (0, 0)
    full = lambda a: pl.BlockSpec(a.shape, fixed)
    wo = w_out.astype(BF16)
    woa, wor = wo[:wa], wo[wa:]
    wq = peer_w_q.astype(BF16)
    k1, k2 = keys_1.astype(BF16), keys_2.astype(BF16)
    vec = lambda a: a.reshape(1, -1)
    return pl.pallas_call(
        _post_mix_kernel,
        grid=(T // tm,),
        in_specs=[pl.BlockSpec((tm, D), row), pl.BlockSpec((tm, wa), row), pl.BlockSpec((tm, wr), row),
                  full(woa), full(wor), pl.BlockSpec((1, D), fixed), pl.BlockSpec((1, D), fixed),
                  pl.BlockSpec((1, D), fixed), full(wq), full(k1), full(k2)],
        out_specs=[pl.BlockSpec((tm, D), row),
                   pl.BlockSpec((2 * PEER_HEADS, tm // LANES, PEER_N_KEYS, LANES), lambda i: (0, i, 0, 0))],
        out_shape=[jax.ShapeDtypeStruct((T, D), F32),
                   jax.ShapeDtypeStruct((2 * PEER_HEADS, T // LANES, PEER_N_KEYS, LANES), F32)],
        compiler_params=_params("parallel"),
        name="post_mix",
    )(x2, mixed_attn, mixed_rnn, woa, wor, vec(b_out), vec(ln1_g), vec(ln1_b), wq, k1, k2)


def _before(a, b):
    return (a[0] > b[0]) | ((a[0] == b[0]) & (a[1] < b[1]))


def _pick(cond, a, b):
    return tuple(jnp.where(cond, x, y) for x, y in zip(a, b))


def _compare_exchange(xs, i, j):
    swap = _before(xs[j], xs[i])
    xs[i], xs[j] = _pick(swap, xs[j], xs[i]), _pick(swap, xs[i], xs[j])


def _batcher_pairs(n):
    pairs, p = [], 1
    while p < n:
        k = p
        while k >= 1:
            for j in range(k % p, n - k, 2 * k):
                for i in range(min(k, n - j - k)):
                    if (i + j) // (2 * p) == (i + j + k) // (2 * p):
                        pairs.append((i + j, i + j + k))
            k //= 2
        p *= 2
    return pairs


def _sort(xs):
    xs = list(xs)
    for i, j in _batcher_pairs(len(xs)):
        _compare_exchange(xs, i, j)
    return xs


def _best_of_two_sorted(a, b, sort_result=True):
    n = len(a)
    c = [_pick(_before(b[n - 1 - i], a[i]), b[n - 1 - i], a[i]) for i in range(n)]
    if sort_result:
        d = n // 2
        while d >= 1:
            for i in range(n):
                if not i & d:
                    _compare_exchange(c, i, i + d)
            d //= 2
    return c


def _peer_topk_kernel(s_ref, e_ref, g_ref, km, sv, si):
    K = PEER_TOPK
    n_groups = PEER_N_KEYS // K
    tile = lambda fill: jnp.full((SUBLANES, LANES), fill, jnp.int32)

    def head(hd, carry):
        for half in range(2):
            for key in range(PEER_N_KEYS):
                km[key] = s_ref[2 * hd + half, :, key, :]

            def sort_group(g, c, half=half):
                xs = _sort([(km[g * K + r], tile(g * K + r)) for r in range(K)])
                for r in range(K):
                    sv[half, g * K + r], si[half, g * K + r] = xs[r]
                return c

            def merge(m, c, half=half):
                read = lambda slot: [(sv[half, slot * K + r], si[half, slot * K + r]) for r in range(K)]
                xs = _best_of_two_sorted(read(2 * m), read(2 * m + 1))
                for r in range(K):
                    sv[half, (n_groups + m) * K + r], si[half, (n_groups + m) * K + r] = xs[r]
                return c

            lax.fori_loop(0, n_groups, sort_group, 0)
            lax.fori_loop(0, n_groups - 1, merge, 0)

        last = (2 * n_groups - 2) * K
        v1 = [(sv[0, last + r], si[0, last + r]) for r in range(K)]
        v2 = [(sv[1, last + r], si[1, last + r]) for r in range(K)]
        def cand(i, j):
            return (v1[i][0] + v2[j][0], tile(i * K + j), v1[i][1] * PEER_N_KEYS + v2[j][1])
        rows = [[cand(i, j) for j in range(K // (i + 1))] for i in range(K)]
        assert K == 16
        g1 = _sort(rows[1] + rows[2] + rows[4])
        g2 = _sort(rows[3] + rows[5] + rows[6] + rows[7] + [r[0] for r in rows[8:14]])
        top = _best_of_two_sorted(_best_of_two_sorted(rows[0], g1), g2)
        top[K - 1] = _pick(_before(rows[14][0], top[K - 1]), rows[14][0], top[K - 1])
        top[K - 2] = _pick(_before(rows[15][0], top[K - 2]), rows[15][0], top[K - 2])
        m = top[0][0]
        for t in top[1:]:
            m = jnp.maximum(m, t[0])
        p = [jnp.exp(t[0] - m) for t in top]
        denom = p[0]
        for x in p[1:]:
            denom = denom + x
        for r in range(K):
            e_ref[hd * K + r] = top[r][2]
            g_ref[hd * K + r] = p[r] / denom
        return carry

    lax.fori_loop(0, PEER_HEADS, head, 0)


def _peer_topk(scores):
    n_sets, n_blocks, n_keys, _ = scores.shape
    T = n_blocks * LANES
    E = PEER_HEADS * PEER_TOPK
    out_map = lambda i: (0, i, 0)
    e3, g3 = pl.pallas_call(
        _peer_topk_kernel,
        grid=(n_blocks // SUBLANES,),
        in_specs=[pl.BlockSpec((n_sets, SUBLANES, n_keys, LANES), lambda i: (0, i, 0, 0))],
        out_specs=[pl.BlockSpec((E, SUBLANES, LANES), out_map), pl.BlockSpec((E, SUBLANES, LANES), out_map)],
        out_shape=[jax.ShapeDtypeStruct((E, n_blocks, LANES), jnp.int32),
                   jax.ShapeDtypeStruct((E, n_blocks, LANES), F32)],
        scratch_shapes=[pltpu.VMEM((n_keys, SUBLANES, LANES), F32),
                        pltpu.VMEM((2, (2 * n_keys // PEER_TOPK - 1) * PEER_TOPK, SUBLANES, LANES), F32),
                        pltpu.VMEM((2, (2 * n_keys // PEER_TOPK - 1) * PEER_TOPK, SUBLANES, LANES), jnp.int32)],
        compiler_params=_params("parallel"),
        name="peer_topk",
    )(scores)
    return e3.reshape(E, T).T, g3.reshape(E, T).T


def _sum_over_sublanes(p, lo4, lo2, lo1):
    z = [jnp.where(lo4, p[k], p[k + 4]) + pltpu.roll(jnp.where(lo4, p[k + 4], p[k]), 4, axis=0) for k in range(4)]
    y = [jnp.where(lo2, z[k] + pltpu.roll(z[k], SUBLANES - 2, axis=0), z[k + 2] + pltpu.roll(z[k + 2], 2, axis=0))
         for k in range(2)]
    return jnp.where(lo1, y[0] + pltpu.roll(y[0], SUBLANES - 1, axis=0), y[1] + pltpu.roll(y[1], 1, axis=0))


def _peer_apply_kernel(ex_ref, exn_ref, g_ref, h_ref, ln_g_ref, ln_b_ref, uv_hbm, o_ref, buf, pbuf0, pbuf1, wbuf0, wbuf1, zbuf, sem):
    i = pl.program_id(0)
    n = pl.num_programs(0)
    tk, E = ex_ref.shape
    slot = lax.rem(i, 2)
    nslot = 1 - slot
    pbuf, wbuf = (pbuf0, pbuf1), (wbuf0, wbuf1)

    def row_copy(e, s, t, j):
        return pltpu.make_async_copy(uv_hbm.at[e], buf.at[s * tk + t, j], sem.at[s, t])

    def start_rows(idx_ref, s, t, j0, j1):
        for j in range(j0, j1):
            row_copy(idx_ref[t, j], s, t, j).start(priority=j % 2)

    def start_token(idx_ref, s, t):
        start_rows(idx_ref, s, t, 0, E)

    def wait_token(s, t):
        pltpu.make_async_copy(buf.at[s * tk + t], buf.at[s * tk + t], sem.at[s, t]).wait()

    @pl.when(i == 0)
    def _():
        def body(t, carry):
            start_token(ex_ref, slot, t)
            return carry
        lax.fori_loop(0, tk, body, 0)

    sub = lax.broadcasted_iota(jnp.int32, (SUBLANES, LANES), 0)
    lo4, lo2, lo1 = sub < 4, (sub & 2) == 0, (sub & 1) == 0
    eye = lax.broadcasted_iota(jnp.int32, (E, LANES), 0) == lax.broadcasted_iota(jnp.int32, (E, LANES), 1)
    n_acc = 4

    groups = E // SUBLANES
    per_chunk = E // groups

    def pipeline(slot, nslot):
        def dots_group(t, g):
            h = h_ref[t]
            base = slot * tk + t
            return _sum_over_sublanes([buf[base, g * SUBLANES + k, 0:SUBLANES, :] * h for k in range(SUBLANES)],
                                      lo4, lo2, lo1)

        def weights(t, par):
            act = jnp.sum(pbuf[par][...], axis=1, keepdims=True)
            gate = jnp.sum(jnp.where(eye, g_ref[t:t + 1, :], 0.0), axis=1, keepdims=True)
            return jnp.broadcast_to(gate * _gelu(act), (E, LANES))

        def mix_group(t, par, g, acc):
            base = slot * tk + t
            for e in range(g * SUBLANES, (g + 1) * SUBLANES):
                w = jnp.broadcast_to(wbuf[par][e:e + 1, :], (SUBLANES, LANES))
                term = w * buf[base, e, SUBLANES:2 * SUBLANES, :]
                acc[e % n_acc] = term if acc[e % n_acc] is None else acc[e % n_acc] + term

        def stage(t, with_dots, with_weights, with_mix=True):
            par = t % 2
            if with_dots:
                wait_token(slot, t + 2)
            w_next = weights(t + 1, 1 - par) if with_weights else None
            acc, pending = [None] * n_acc, None
            for g in range(groups):
                if with_mix:
                    start_rows(exn_ref, nslot, t, g * per_chunk, (g + 1) * per_chunk)
                if with_dots:
                    p = dots_group(t + 2, g)
                    if pending is not None:
                        pbuf[par][(g - 1) * SUBLANES:g * SUBLANES, :] = pending
                    pending = p
                if with_mix:
                    mix_group(t, par, g, acc)
            if with_dots:
                pbuf[par][(groups - 1) * SUBLANES:, :] = pending
            if with_mix:
                zbuf[t] = ALPHA * h_ref[t] + ((acc[0] + acc[1]) + (acc[2] + acc[3]))
            if with_weights:
                wbuf[1 - par][...] = w_next

        assert tk >= 2
        stage(-2, True, False, with_mix=False)
        stage(-1, True, True, with_mix=False)
        for t in range(tk):
            stage(t, t + 2 < tk, t + 1 < tk)

    for static_slot in range(2):
        pl.when(slot == static_slot)(functools.partial(pipeline, static_slot, 1 - static_slot))


    z = zbuf[...]
    total = lambda a: jnp.sum(jnp.sum(a, axis=2, keepdims=True), axis=1, keepdims=True)
    inv_d = 1.0 / (SUBLANES * LANES)
    zc = z - total(z) * inv_d
    var = total(zc * zc) * inv_d
    o_ref[...] = zc * lax.rsqrt(var + EPS) * ln_g_ref[...] + ln_b_ref[...]

    @pl.when(i == n - 1)
    def _():
        def body(t, carry):
            wait_token(nslot, t)
            return carry
        lax.fori_loop(0, tk, body, 0)


def _peer_apply(h1, experts, gates, peer_u, peer_v, ln2_g, ln2_b, tk=16):
    T, D = h1.shape
    E = experts.shape[1]
    assert D == SUBLANES * LANES and E == LANES
    n = T // tk
    tile = lambda a: a.reshape(-1, SUBLANES, LANES)
    uv = jnp.concatenate([tile(peer_u), tile(peer_v)], axis=1)
    row = lambda i: (i, 0)
    nxt = lambda i: (jnp.minimum(i + 1, n - 1), 0)
    row3 = lambda i: (i, 0, 0)
    fixed = lambda i: (0, 0)
    out = pl.pallas_call(
        _peer_apply_kernel,
        grid=(n,),
        in_specs=[pl.BlockSpec((tk, E), row, memory_space=pltpu.SMEM),
                  pl.BlockSpec((tk, E), nxt, memory_space=pltpu.SMEM),
                  pl.BlockSpec((tk, E), row), pl.BlockSpec((tk, SUBLANES, LANES), row3),
                  pl.BlockSpec((SUBLANES, LANES), fixed), pl.BlockSpec((SUBLANES, LANES), fixed),
                  pl.BlockSpec(memory_space=pl.ANY)],
        out_specs=pl.BlockSpec((tk, SUBLANES, LANES), row3),
        out_shape=jax.ShapeDtypeStruct((T, SUBLANES, LANES), F32),
        scratch_shapes=[pltpu.VMEM((2 * tk, E, 2 * SUBLANES, LANES), F32)]
                       + [pltpu.VMEM((E, LANES), F32)] * 4
                       + [pltpu.VMEM((tk, SUBLANES, LANES), F32), pltpu.SemaphoreType.DMA((2, tk))],
        compiler_params=_params("arbitrary"),
        name="peer_apply",
    )(experts, experts, gates, tile(h1), ln2_g.reshape(SUBLANES, LANES), ln2_b.reshape(SUBLANES, LANES), uv)
    return out.reshape(T, D)


def kernel(x, w_in, b_in, attn_sinks, conv_w, conv_b, gate_a_w, gate_a_b, gate_x_w, gate_x_b, lru_lambda, norm_attn_g, norm_rnn_g, w_out, b_out, ln1_g, ln1_b, peer_w_q, peer_keys_1, peer_keys_2, peer_u, peer_v, ln2_g, ln2_b):
    B, S, D = x.shape
    x2 = x.reshape(B * S, D)
    rnn_width = conv_w.shape[1]
    q, k, v, xr, gate_in = _in_proj(x2, w_in, b_in, rnn_width)
    mixed_attn = _attention(q, k, v, attn_sinks, norm_attn_g, B, S).reshape(B * S, ATTN_WIDTH)
    mixed_rnn = _rglru(xr, gate_in, conv_w, conv_b, gate_a_w, gate_a_b, gate_x_w, gate_x_b, lru_lambda,
                       norm_rnn_g, B, S).reshape(B * S, rnn_width)
    h1, scores_t = _post_mix(x2, mixed_attn, mixed_rnn, w_out, b_out, ln1_g, ln1_b, peer_w_q,
                             peer_keys_1, peer_keys_2)
    experts, gates = _peer_topk(scores_t)
    out = _peer_apply(h1, experts, gates, peer_u, peer_v, ln2_g, ln2_b)
    return out.reshape(B, S, D)
```

```python
import functools
import math

import jax
import jax.numpy as jnp
from jax import lax
from jax.experimental import pallas as pl
from jax.experimental.pallas import tpu as pltpu

F32 = jnp.float32
BF16 = jnp.bfloat16

ATTN_HEADS = 8
KV_HEADS = 2
HEAD_DIM = 64
GQA_GROUP = ATTN_HEADS // KV_HEADS
ATTN_WIDTH = ATTN_HEADS * HEAD_DIM
KV_WIDTH = KV_HEADS * HEAD_DIM
WINDOW = 128
ATTN_BLOCK = 128
ROPE_DIM = HEAD_DIM // 4
ROPE_THETA = 500000.0
RNN_HEADS = 8
CONV_WIDTH = 4
LRU_C = 8.0
PEER_HEADS = 8
PEER_N_KEYS = 128
PEER_HALF_DIM = 128
PEER_TOPK = 16
DEPTH = 1
ALPHA = (2.0 * DEPTH) ** 0.25
EPS = 1e-5

LANES = 128
SUBLANES = 8
VMEM_LIMIT_BYTES = 56 * 1024 * 1024


def _params(*semantics):
    return pltpu.CompilerParams(dimension_semantics=semantics, vmem_limit_bytes=VMEM_LIMIT_BYTES)


def _gelu(x):
    c = math.sqrt(2.0 / math.pi)
    return 0.5 * x * (1.0 + jnp.tanh(c * (x + 0.044715 * (x * x * x))))


def _layer_norm(x, g, b):
    mu = jnp.mean(x, axis=-1, keepdims=True)
    xc = x - mu
    var = jnp.mean(xc * xc, axis=-1, keepdims=True)
    return xc * lax.rsqrt(var + EPS) * g + b


def _rms_norm(x, g):
    return x * lax.rsqrt(jnp.mean(x * x, axis=-1, keepdims=True) + EPS) * g


def _in_proj_kernel(x_ref, w_ref, b_ref, q_ref, k_ref, v_ref, xr_ref, g_ref):
    z = jnp.dot(x_ref[...].astype(BF16), w_ref[...], preferred_element_type=F32) + b_ref[...]
    c0 = q_ref.shape[1]
    c1 = c0 + k_ref.shape[1]
    c2 = c1 + v_ref.shape[1]
    c3 = c2 + xr_ref.shape[1]
    q_ref[...] = z[:, :c0]
    k_ref[...] = z[:, c0:c1]
    v_ref[...] = z[:, c1:c2]
    xr_ref[...] = z[:, c2:c3]
    g_ref[...] = z[:, c3:]


def _in_proj(x2, w_in, b_in, rnn_width, tm=512):
    T, D = x2.shape
    n_in = w_in.shape[1]
    widths = (ATTN_WIDTH, KV_WIDTH, KV_WIDTH, rnn_width, rnn_width)
    row = lambda i: (i, 0)
    fixed = lambda i: (0, 0)
    return pl.pallas_call(
        _in_proj_kernel,
        grid=(T // tm,),
        in_specs=[pl.BlockSpec((tm, D), row), pl.BlockSpec((D, n_in), fixed), pl.BlockSpec((1, n_in), fixed)],
        out_specs=[pl.BlockSpec((tm, w), row) for w in widths],
        out_shape=[jax.ShapeDtypeStruct((T, w), F32) for w in widths],
        compiler_params=_params("parallel"),
        name="in_proj",
    )(x2, w_in.astype(BF16), b_in.reshape(1, n_in))


def _rotary(t, cos_t, sin_lo, sin_hi):
    half = ROPE_DIM // 2
    cols = []
    for c in range(t.shape[1] // LANES):
        tc = t[:, c * LANES:(c + 1) * LANES]
        cols.append(tc * cos_t + pltpu.roll(tc, LANES - half, axis=1) * sin_lo + pltpu.roll(tc, half, axis=1) * sin_hi)
    return cols[0] if len(cols) == 1 else jnp.concatenate(cols, axis=1)


def _attention_kernel(sink_ref, q_ref, kc_ref, kp_ref, vc_ref, vp_ref, cos_ref, slo_ref, shi_ref,
                      cosp_ref, slop_ref, ship_ref, g_ref, o_ref):
    n = pl.program_id(1)
    blk = ATTN_BLOCK
    q = _rotary(q_ref[0], cos_ref[...], slo_ref[...], shi_ref[...]) * (HEAD_DIM ** -0.5)
    kc = _rotary(kc_ref[0], cos_ref[...], slo_ref[...], shi_ref[...])
    kp = _rotary(kp_ref[0], cosp_ref[...], slop_ref[...], ship_ref[...])
    kw = jnp.concatenate([kp, kc], axis=0).astype(BF16)
    vw = jnp.concatenate([vp_ref[0], vc_ref[0]], axis=0).astype(BF16)
    qb = q.astype(BF16)

    qi = lax.broadcasted_iota(jnp.int32, (blk, 2 * blk), 0)
    ki = lax.broadcasted_iota(jnp.int32, (blk, 2 * blk), 1)
    diff = blk + qi - ki
    first_key = jnp.where(n == 0, blk, 0)
    valid = (diff >= 0) & (diff < WINDOW) & (ki >= first_key)

    outs = []
    for h in range(ATTN_HEADS):
        kv = h // GQA_GROUP
        qh = qb[:, h * HEAD_DIM:(h + 1) * HEAD_DIM]
        kh = kw[:, kv * HEAD_DIM:(kv + 1) * HEAD_DIM]
        vh = vw[:, kv * HEAD_DIM:(kv + 1) * HEAD_DIM]
        s = lax.dot_general(qh, kh, (((1,), (1,)), ((), ())), preferred_element_type=F32)
        s = jnp.where(valid, s, -jnp.inf)
        sink = sink_ref[h]
        m = jnp.maximum(jnp.max(s, axis=-1, keepdims=True), sink)
        p = jnp.exp(s - m)
        denom = jnp.sum(p, axis=-1, keepdims=True) + jnp.exp(sink - m)
        probs = (p / denom).astype(BF16)
        outs.append(jnp.dot(probs, vh, preferred_element_type=F32))
    o = jnp.concatenate(outs, axis=1)
    o_ref[0] = _rms_norm(o, g_ref[...])


def _rope_tables(S):
    half = ROPE_DIM // 2
    pos = jnp.arange(S, dtype=F32)
    inv_freq = ROPE_THETA ** (-jnp.arange(0, ROPE_DIM, 2, dtype=F32) / ROPE_DIM)
    ang = pos[:, None] * inv_freq[None, :]
    cos, sin = jnp.cos(ang), jnp.sin(ang)
    ones = jnp.ones((S, HEAD_DIM - ROPE_DIM), F32)
    zeros = jnp.zeros((S, HEAD_DIM - ROPE_DIM), F32)
    zh = jnp.zeros((S, half), F32)
    per_head = lambda *parts: jnp.tile(jnp.concatenate(parts, axis=1), (1, LANES // HEAD_DIM))
    return per_head(cos, cos, ones), per_head(-sin, zh, zeros), per_head(zh, sin, zeros)


def _attention(q, k, v, sinks, norm_g, B, S):
    blk = ATTN_BLOCK
    nb = S // blk
    cos_t, sin_lo, sin_hi = _rope_tables(S)
    cur = lambda b, n: (b, n, 0)
    prev = lambda b, n: (b, jnp.maximum(n - 1, 0), 0)
    tcur = lambda b, n: (n, 0)
    tprev = lambda b, n: (jnp.maximum(n - 1, 0), 0)
    tab = lambda im: pl.BlockSpec((blk, LANES), im)
    return pl.pallas_call(
        _attention_kernel,
        grid=(B, nb),
        in_specs=[pl.BlockSpec(memory_space=pltpu.SMEM),
                  pl.BlockSpec((1, blk, ATTN_WIDTH), cur),
                  pl.BlockSpec((1, blk, KV_WIDTH), cur), pl.BlockSpec((1, blk, KV_WIDTH), prev),
                  pl.BlockSpec((1, blk, KV_WIDTH), cur), pl.BlockSpec((1, blk, KV_WIDTH), prev),
                  tab(tcur), tab(tcur), tab(tcur), tab(tprev), tab(tprev), tab(tprev),
                  pl.BlockSpec((1, ATTN_WIDTH), lambda b, n: (0, 0))],
        out_specs=pl.BlockSpec((1, blk, ATTN_WIDTH), cur),
        out_shape=jax.ShapeDtypeStruct((B, S, ATTN_WIDTH), F32),
        compiler_params=_params("parallel", "parallel"),
        name="attention",
    )(sinks, q.reshape(B, S, ATTN_WIDTH), k.reshape(B, S, KV_WIDTH), k.reshape(B, S, KV_WIDTH),
      v.reshape(B, S, KV_WIDTH), v.reshape(B, S, KV_WIDTH),
      cos_t, sin_lo, sin_hi, cos_t, sin_lo, sin_hi, norm_g.reshape(1, ATTN_WIDTH))


def _rglru_kernel(xr_ref, gin_ref, cw_ref, cb_ref, wg_ref, bg_ref, lam_ref, g_ref, o_ref, tail_ref, h_ref):
    ts, C = xr_ref.shape[1], xr_ref.shape[2]

    @pl.when(pl.program_id(1) == 0)
    def _():
        tail_ref[...] = jnp.zeros_like(tail_ref)
        h_ref[...] = jnp.zeros_like(h_ref)

    xr = xr_ref[0]
    xcat = jnp.concatenate([tail_ref[...], xr], axis=0)
    tail_ref[...] = xr[ts - SUBLANES:, :]
    xc = cb_ref[...] + cw_ref[CONV_WIDTH - 1:CONV_WIDTH, :] * xr
    for j in range(CONV_WIDTH - 1):
        back = CONV_WIDTH - 1 - j
        xc = xc + cw_ref[j:j + 1, :] * pltpu.roll(xcat, back, axis=0)[SUBLANES:, :]

    gates = jnp.dot(xc.astype(BF16), wg_ref[...], preferred_element_type=F32) + bg_ref[...]
    r = jax.nn.sigmoid(gates[:, :C])
    i = jax.nn.sigmoid(gates[:, C:])
    lam = lam_ref[...]
    softplus_neg_lam = jnp.maximum(-lam, 0.0) + jnp.log1p(jnp.exp(-jnp.abs(lam)))
    log_a = -LRU_C * r * softplus_neg_lam
    a = jnp.exp(log_a)
    b = jnp.sqrt((1.0 + jnp.exp(2.0 * log_a)) * jnp.tanh(-log_a)) * (i * xc)

    row = lax.broadcasted_iota(jnp.int32, (ts, C), 0)
    d = 1
    while d < ts:
        keep = row >= d
        b = jnp.where(keep, a * pltpu.roll(b, d, axis=0) + b, b)
        a = jnp.where(keep, a * pltpu.roll(a, d, axis=0), a)
        d *= 2
    h = b + a * h_ref[0:1, :]
    h_ref[...] = jnp.broadcast_to(h[ts - 1:ts, :], h_ref.shape)
    o_ref[0] = _rms_norm(_gelu(gin_ref[0]) * h, g_ref[...])


def _block_diag(w):
    H, a, b = w.shape
    eye = jnp.eye(H, dtype=w.dtype)
    return (w[:, :, None, :] * eye[:, None, :, None]).reshape(H * a, H * b)


def _rglru(xr, gate_in, conv_w, conv_b, gate_a_w, gate_a_b, gate_x_w, gate_x_b, lru_lambda, norm_g, B, S, ts=256):
    C = xr.shape[-1]
    wg = jnp.concatenate([_block_diag(gate_a_w), _block_diag(gate_x_w)], axis=1).astype(BF16)
    bg = jnp.concatenate([gate_a_b, gate_x_b]).reshape(1, 2 * C)
    cur = lambda b, n: (b, n, 0)
    fixed = lambda b, n: (0, 0)
    return pl.pallas_call(
        _rglru_kernel,
        grid=(B, S // ts),
        in_specs=[pl.BlockSpec((1, ts, C), cur), pl.BlockSpec((1, ts, C), cur),
                  pl.BlockSpec((CONV_WIDTH, C), fixed), pl.BlockSpec((1, C), fixed),
                  pl.BlockSpec((C, 2 * C), fixed), pl.BlockSpec((1, 2 * C), fixed),
                  pl.BlockSpec((1, C), fixed), pl.BlockSpec((1, C), fixed)],
        out_specs=pl.BlockSpec((1, ts, C), cur),
        out_shape=jax.ShapeDtypeStruct((B, S, C), F32),
        scratch_shapes=[pltpu.VMEM((SUBLANES, C), F32), pltpu.VMEM((SUBLANES, C), F32)],
        compiler_params=_params("parallel", "arbitrary"),
        name="rglru",
    )(xr.reshape(B, S, C), gate_in.reshape(B, S, C), conv_w, conv_b.reshape(1, C), wg, bg,
      lru_lambda.reshape(1, C), norm_g.reshape(1, C))


def _post_mix_kernel(x_ref, ma_ref, mr_ref, woa_ref, wor_ref, bo_ref, g1_ref, b1_ref, wq_ref, k1_ref, k2_ref,
                     h_ref, s_ref):
    y = (jnp.dot(ma_ref[...].astype(BF16), woa_ref[...], preferred_element_type=F32)
         + jnp.dot(mr_ref[...].astype(BF16), wor_ref[...], preferred_element_type=F32) + bo_ref[...])
    h = _layer_norm(ALPHA * x_ref[...] + y, g1_ref[...], b1_ref[...])
    h_ref[...] = h
    q = jnp.dot(h.astype(BF16), wq_ref[...], preferred_element_type=F32).astype(BF16)
    nt = (((1,), (1,)), ((), ()))
    for hd in range(PEER_HEADS):
        for half, k_ref in enumerate((k1_ref, k2_ref)):
            c = (hd * 2 + half) * PEER_HALF_DIM
            for blk in range(s_ref.shape[1]):
                qb = q[blk * LANES:(blk + 1) * LANES, c:c + PEER_HALF_DIM]
                s_ref[hd * 2 + half, blk] = lax.dot_general(k_ref[...], qb, nt, preferred_element_type=F32)


def _post_mix(x2, mixed_attn, mixed_rnn, w_out, b_out, ln1_g, ln1_b, peer_w_q, keys_1, keys_2, tm=512):
    T, D = x2.shape
    wa, wr = mixed_attn.shape[1], mixed_rnn.shape[1]
    nq = peer_w_q.shape[1]
    row = lambda i: (i, 0)
    fixed = lambda i: (0, 0)
    full = lambda a: pl.BlockSpec(a.shape, fixed)
    wo = w_out.astype(BF16)
    woa, wor = wo[:wa], wo[wa:]
    wq = peer_w_q.astype(BF16)
    k1, k2 = keys_1.astype(BF16), keys_2.astype(BF16)
    vec = lambda a: a.reshape(1, -1)
    return pl.pallas_call(
        _post_mix_kernel,
        grid=(T // tm,),
        in_specs=[pl.BlockSpec((tm, D), row), pl.BlockSpec((tm, wa), row), pl.BlockSpec((tm, wr), row),
                  full(woa), full(wor), pl.BlockSpec((1, D), fixed), pl.BlockSpec((1, D), fixed),
                  pl.BlockSpec((1, D), fixed), full(wq), full(k1), full(k2)],
        out_specs=[pl.BlockSpec((tm, D), row),
                   pl.BlockSpec((2 * PEER_HEADS, tm // LANES, PEER_N_KEYS, LANES), lambda i: (0, i, 0, 0))],
        out_shape=[jax.ShapeDtypeStruct((T, D), F32),
                   jax.ShapeDtypeStruct((2 * PEER_HEADS, T // LANES, PEER_N_KEYS, LANES), F32)],
        compiler_params=_params("parallel"),
        name="post_mix",
    )(x2, mixed_attn, mixed_rnn, woa, wor, vec(b_out), vec(ln1_g), vec(ln1_b), wq, k1, k2)


def _before(a, b):
    return (a[0] > b[0]) | ((a[0] == b[0]) & (a[1] < b[1]))


def _pick(cond, a, b):
    return tuple(jnp.where(cond, x, y) for x, y in zip(a, b))


def _compare_exchange(xs, i, j):
    swap = _before(xs[j], xs[i])
    xs[i], xs[j] = _pick(swap, xs[j], xs[i]), _pick(swap, xs[i], xs[j])


def _batcher_pairs(n):
    pairs, p = [], 1
    while p < n:
        k = p
        while k >= 1:
            for j in range(k % p, n - k, 2 * k):
                for i in range(min(k, n - j - k)):
                    if (i + j) // (2 * p) == (i + j + k) // (2 * p):
                        pairs.append((i + j, i + j + k))
            k //= 2
        p *= 2
    return pairs


def _sort(xs):
    xs = list(xs)
    for i, j in _batcher_pairs(len(xs)):
        _compare_exchange(xs, i, j)
    return xs


def _best_of_two_sorted(a, b, sort_result=True):
    n = len(a)
    c = [_pick(_before(b[n - 1 - i], a[i]), b[n - 1 - i], a[i]) for i in range(n)]
    if sort_result:
        d = n // 2
        while d >= 1:
            for i in range(n):
                if not i & d:
                    _compare_exchange(c, i, i + d)
            d //= 2
    return c


def _peer_topk_kernel(s_ref, e_ref, g_ref, km, sv, si):
    K = PEER_TOPK
    n_groups = PEER_N_KEYS // K
    tile = lambda fill: jnp.full((SUBLANES, LANES), fill, jnp.int32)

    def head(hd, carry):
        for half in range(2):
            for key in range(PEER_N_KEYS):
                km[key] = s_ref[2 * hd + half, :, key, :]

            def sort_group(g, c, half=half):
                xs = _sort([(km[g * K + r], tile(g * K + r)) for r in range(K)])
                for r in range(K):
                    sv[half, g * K + r], si[half, g * K + r] = xs[r]
                return c

            def merge(m, c, half=half):
                read = lambda slot: [(sv[half, slot * K + r], si[half, slot * K + r]) for r in range(K)]
                xs = _best_of_two_sorted(read(2 * m), read(2 * m + 1))
                for r in range(K):
                    sv[half, (n_groups + m) * K + r], si[half, (n_groups + m) * K + r] = xs[r]
                return c

            lax.fori_loop(0, n_groups, sort_group, 0)
            lax.fori_loop(0, n_groups - 1, merge, 0)

        last = (2 * n_groups - 2) * K
        v1 = [(sv[0, last + r], si[0, last + r]) for r in range(K)]
        v2 = [(sv[1, last + r], si[1, last + r]) for r in range(K)]
        def cand(i, j):
            return (v1[i][0] + v2[j][0], tile(i * K + j), v1[i][1] * PEER_N_KEYS + v2[j][1])
        rows = [[cand(i, j) for j in range(K // (i + 1))] for i in range(K)]
        assert K == 16
        g1 = _sort(rows[1] + rows[2] + rows[4])
        g2 = _sort(rows[3] + rows[5] + rows[6] + rows[7] + [r[0] for r in rows[8:14]])
        top = _best_of_two_sorted(_best_of_two_sorted(rows[0], g1), g2)
        top[K - 1] = _pick(_before(rows[14][0], top[K - 1]), rows[14][0], top[K - 1])
        top[K - 2] = _pick(_before(rows[15][0], top[K - 2]), rows[15][0], top[K - 2])
        m = top[0][0]
        for t in top[1:]:
            m = jnp.maximum(m, t[0])
        p = [jnp.exp(t[0] - m) for t in top]
        denom = p[0]
        for x in p[1:]:
            denom = denom + x
        for r in range(K):
            e_ref[hd * K + r] = top[r][2]
            g_ref[hd * K + r] = p[r] / denom
        return carry

    lax.fori_loop(0, PEER_HEADS, head, 0)


def _peer_topk(scores):
    n_sets, n_blocks, n_keys, _ = scores.shape
    T = n_blocks * LANES
    E = PEER_HEADS * PEER_TOPK
    out_map = lambda i: (0, i, 0)
    e3, g3 = pl.pallas_call(
        _peer_topk_kernel,
        grid=(n_blocks // SUBLANES,),
        in_specs=[pl.BlockSpec((n_sets, SUBLANES, n_keys, LANES), lambda i: (0, i, 0, 0))],
        out_specs=[pl.BlockSpec((E, SUBLANES, LANES), out_map), pl.BlockSpec((E, SUBLANES, LANES), out_map)],
        out_shape=[jax.ShapeDtypeStruct((E, n_blocks, LANES), jnp.int32),
                   jax.ShapeDtypeStruct((E, n_blocks, LANES), F32)],
        scratch_shapes=[pltpu.VMEM((n_keys, SUBLANES, LANES), F32),
                        pltpu.VMEM((2, (2 * n_keys // PEER_TOPK - 1) * PEER_TOPK, SUBLANES, LANES), F32),
                        pltpu.VMEM((2, (2 * n_keys // PEER_TOPK - 1) * PEER_TOPK, SUBLANES, LANES), jnp.int32)],
        compiler_params=_params("parallel"),
        name="peer_topk",
    )(scores)
    return e3.reshape(E, T).T, g3.reshape(E, T).T


def _sum_over_sublanes(p, lo4, lo2, lo1):
    z = [jnp.where(lo4, p[k], p[k + 4]) + pltpu.roll(jnp.where(lo4, p[k + 4], p[k]), 4, axis=0) for k in range(4)]
    y = [jnp.where(lo2, z[k] + pltpu.roll(z[k], SUBLANES - 2, axis=0), z[k + 2] + pltpu.roll(z[k + 2], 2, axis=0))
         for k in range(2)]
    return jnp.where(lo1, y[0] + pltpu.roll(y[0], SUBLANES - 1, axis=0), y[1] + pltpu.roll(y[1], 1, axis=0))


def _peer_apply_kernel(ex_ref, exn_ref, g_ref, h2d_ref, ln_g_ref, ln_b_ref, uv_hbm, o_ref, buf, pbuf0, pbuf1, wbuf0, wbuf1,
                       h_ref, zbuf, sem):
    i = pl.program_id(0)
    n = pl.num_programs(0)
    tk, E = ex_ref.shape
    slot = lax.rem(i, 2)
    nslot = 1 - slot
    pbuf, wbuf = (pbuf0, pbuf1), (wbuf0, wbuf1)
    for c in range(SUBLANES):
        h_ref[:, c, :] = h2d_ref[:, c * LANES:(c + 1) * LANES]

    def row_copy(e, s, t, j):
        return pltpu.make_async_copy(uv_hbm.at[e], buf.at[s * tk + t, j], sem.at[s, t])

    def start_rows(idx_ref, s, t, j0, j1):
        for j in range(j0, j1):
            row_copy(idx_ref[t, j], s, t, j).start(priority=j % 2)

    def start_token(idx_ref, s, t):
        start_rows(idx_ref, s, t, 0, E)

    def wait_token(s, t):
        pltpu.make_async_copy(buf.at[s * tk + t], buf.at[s * tk + t], sem.at[s, t]).wait()

    @pl.when(i == 0)
    def _():
        def body(t, carry):
            start_token(ex_ref, slot, t)
            return carry
        lax.fori_loop(0, tk, body, 0)

    sub = lax.broadcasted_iota(jnp.int32, (SUBLANES, LANES), 0)
    lo4, lo2, lo1 = sub < 4, (sub & 2) == 0, (sub & 1) == 0
    eye = lax.broadcasted_iota(jnp.int32, (E, LANES), 0) == lax.broadcasted_iota(jnp.int32, (E, LANES), 1)
    n_acc = 4

    groups = E // SUBLANES
    per_chunk = E // groups

    def dots_group(t, g):
        h = h_ref[t]
        base = slot * tk + t
        return _sum_over_sublanes([buf[base, g * SUBLANES + k, 0:SUBLANES, :] * h for k in range(SUBLANES)],
                                  lo4, lo2, lo1)

    def weights(t, par):
        act = jnp.sum(pbuf[par][...], axis=1, keepdims=True)
        gate = jnp.sum(jnp.where(eye, g_ref[pl.ds(t, 1), :], 0.0), axis=1, keepdims=True)
        return jnp.broadcast_to(gate * _gelu(act), (E, LANES))

    def mix_group(t, par, g, acc):
        base = slot * tk + t
        for e in range(g * SUBLANES, (g + 1) * SUBLANES):
            w = jnp.broadcast_to(wbuf[par][e:e + 1, :], (SUBLANES, LANES))
            term = w * buf[base, e, SUBLANES:2 * SUBLANES, :]
            acc[e % n_acc] = term if acc[e % n_acc] is None else acc[e % n_acc] + term

    def stage(t, par, with_dots, with_weights, with_mix=True):
        if with_dots:
            wait_token(slot, t + 2)
        w_next = weights(t + 1, 1 - par) if with_weights else None
        acc, pending = [None] * n_acc, None
        for g in range(groups):
            if with_mix:
                start_rows(exn_ref, nslot, t, g * per_chunk, (g + 1) * per_chunk)
            if with_dots:
                p = dots_group(t + 2, g)
                if pending is not None:
                    pbuf[par][(g - 1) * SUBLANES:g * SUBLANES, :] = pending
                pending = p
            if with_mix:
                mix_group(t, par, g, acc)
        if with_dots:
            pbuf[par][(groups - 1) * SUBLANES:, :] = pending
        if with_mix:
            zbuf[t] = ALPHA * h_ref[t] + ((acc[0] + acc[1]) + (acc[2] + acc[3]))
        if with_weights:
            wbuf[1 - par][...] = w_next

    stage(-2, 0, True, False, with_mix=False)
    stage(-1, 1, True, True, with_mix=False)

    def token_pair(tt, carry):
        t = 2 * tt
        stage(t, 0, True, True)
        stage(t + 1, 1, True, True)
        return carry

    assert tk % 2 == 0 and tk >= 4
    lax.fori_loop(0, (tk - 2) // 2, token_pair, 0)

    stage(tk - 2, 0, False, True)
    stage(tk - 1, 1, False, False)


    z = zbuf[...]
    total = lambda a: jnp.sum(jnp.sum(a, axis=2, keepdims=True), axis=1, keepdims=True)
    inv_d = 1.0 / (SUBLANES * LANES)
    zc = z - total(z) * inv_d
    var = total(zc * zc) * inv_d
    zbuf[...] = zc * lax.rsqrt(var + EPS) * ln_g_ref[...] + ln_b_ref[...]
    for c in range(SUBLANES):
        o_ref[:, c * LANES:(c + 1) * LANES] = zbuf[:, c, :]

    @pl.when(i == n - 1)
    def _():
        def body(t, carry):
            wait_token(nslot, t)
            return carry
        lax.fori_loop(0, tk, body, 0)


def _peer_apply(h1, experts, gates, peer_u, peer_v, ln2_g, ln2_b, tk=16):
    T, D = h1.shape
    E = experts.shape[1]
    assert D == SUBLANES * LANES and E == LANES
    n = T // tk
    tile = lambda a: a.reshape(-1, SUBLANES, LANES)
    uv = jnp.concatenate([tile(peer_u), tile(peer_v)], axis=1)
    row = lambda i: (i, 0)
    nxt = lambda i: (jnp.minimum(i + 1, n - 1), 0)
    fixed = lambda i: (0, 0)
    return pl.pallas_call(
        _peer_apply_kernel,
        grid=(n,),
        in_specs=[pl.BlockSpec((tk, E), row, memory_space=pltpu.SMEM),
                  pl.BlockSpec((tk, E), nxt, memory_space=pltpu.SMEM),
                  pl.BlockSpec((tk, E), row), pl.BlockSpec((tk, D), row),
                  pl.BlockSpec((SUBLANES, LANES), fixed), pl.BlockSpec((SUBLANES, LANES), fixed),
                  pl.BlockSpec(memory_space=pl.ANY)],
        out_specs=pl.BlockSpec((tk, D), row),
        out_shape=jax.ShapeDtypeStruct((T, D), F32),
        scratch_shapes=[pltpu.VMEM((2 * tk, E, 2 * SUBLANES, LANES), F32)]
                       + [pltpu.VMEM((E, LANES), F32)] * 4
                       + [pltpu.VMEM((tk, SUBLANES, LANES), F32)] * 2 + [pltpu.SemaphoreType.DMA((2, tk))],
        compiler_params=_params("arbitrary"),
        name="peer_apply",
    )(experts, experts, gates, h1, ln2_g.reshape(SUBLANES, LANES), ln2_b.reshape(SUBLANES, LANES), uv)


def kernel(x, w_in, b_in, attn_sinks, conv_w, conv_b, gate_a_w, gate_a_b, gate_x_w, gate_x_b, lru_lambda, norm_attn_g, norm_rnn_g, w_out, b_out, ln1_g, ln1_b, peer_w_q, peer_keys_1, peer_keys_2, peer_u, peer_v, ln2_g, ln2_b):
    B, S, D = x.shape
    x2 = x.reshape(B * S, D)
    rnn_width = conv_w.shape[1]
    q, k, v, xr, gate_in = _in_proj(x2, w_in, b_in, rnn_width)
    mixed_attn = _attention(q, k, v, attn_sinks, norm_attn_g, B, S).reshape(B * S, ATTN_WIDTH)
    mixed_rnn = _rglru(xr, gate_in, conv_w, conv_b, gate_a_w, gate_a_b, gate_x_w, gate_x_b, lru_lambda,
                       norm_rnn_g, B, S).reshape(B * S, rnn_width)
    h1, scores_t = _post_mix(x2, mixed_attn, mixed_rnn, w_out, b_out, ln1_g, ln1_b, peer_w_q,
                             peer_keys_1, peer_keys_2)
    experts, gates = _peer_topk(scores_t)
    out = _peer_apply(h1, experts, gates, peer_u, peer_v, ln2_g, ln2_b)
    return out.reshape(B, S, D)
```

```python
import functools
import math

import jax
import jax.numpy as jnp
from jax import lax
from jax.experimental import pallas as pl
from jax.experimental.pallas import tpu as pltpu

F32 = jnp.float32
BF16 = jnp.bfloat16

ATTN_HEADS = 8
KV_HEADS = 2
HEAD_DIM = 64
GQA_GROUP = ATTN_HEADS // KV_HEADS
ATTN_WIDTH = ATTN_HEADS * HEAD_DIM
KV_WIDTH = KV_HEADS * HEAD_DIM
WINDOW = 128
ATTN_BLOCK = 128
ROPE_DIM = HEAD_DIM // 4
ROPE_THETA = 500000.0
RNN_HEADS = 8
CONV_WIDTH = 4
LRU_C = 8.0
PEER_HEADS = 8
PEER_N_KEYS = 128
PEER_HALF_DIM = 128
PEER_TOPK = 16
DEPTH = 1
ALPHA = (2.0 * DEPTH) ** 0.25
EPS = 1e-5

LANES = 128
SUBLANES = 8
VMEM_LIMIT_BYTES = 56 * 1024 * 1024


def _params(*semantics):
    return pltpu.CompilerParams(dimension_semantics=semantics, vmem_limit_bytes=VMEM_LIMIT_BYTES)


def _gelu(x):
    c = math.sqrt(2.0 / math.pi)
    return 0.5 * x * (1.0 + jnp.tanh(c * (x + 0.044715 * (x * x * x))))


def _layer_norm(x, g, b):
    mu = jnp.mean(x, axis=-1, keepdims=True)
    xc = x - mu
    var = jnp.mean(xc * xc, axis=-1, keepdims=True)
    return xc * lax.rsqrt(var + EPS) * g + b


def _rms_norm(x, g):
    return x * lax.rsqrt(jnp.mean(x * x, axis=-1, keepdims=True) + EPS) * g


def _in_proj_kernel(x_ref, w_ref, b_ref, q_ref, k_ref, v_ref, xr_ref, g_ref):
    z = jnp.dot(x_ref[...].astype(BF16), w_ref[...], preferred_element_type=F32) + b_ref[...]
    c0 = q_ref.shape[1]
    c1 = c0 + k_ref.shape[1]
    c2 = c1 + v_ref.shape[1]
    c3 = c2 + xr_ref.shape[1]
    q_ref[...] = z[:, :c0]
    k_ref[...] = z[:, c0:c1]
    v_ref[...] = z[:, c1:c2]
    xr_ref[...] = z[:, c2:c3]
    g_ref[...] = z[:, c3:]


def _in_proj(x2, w_in, b_in, rnn_width, tm=512):
    T, D = x2.shape
    n_in = w_in.shape[1]
    widths = (ATTN_WIDTH, KV_WIDTH, KV_WIDTH, rnn_width, rnn_width)
    row = lambda i: (i, 0)
    fixed = lambda i: (0, 0)
    return pl.pallas_call(
        _in_proj_kernel,
        grid=(T // tm,),
        in_specs=[pl.BlockSpec((tm, D), row), pl.BlockSpec((D, n_in), fixed), pl.BlockSpec((1, n_in), fixed)],
        out_specs=[pl.BlockSpec((tm, w), row) for w in widths],
        out_shape=[jax.ShapeDtypeStruct((T, w), F32) for w in widths],
        compiler_params=_params("parallel"),
        name="in_proj",
    )(x2, w_in.astype(BF16), b_in.reshape(1, n_in))


def _rotary(t, cos_t, sin_lo, sin_hi):
    half = ROPE_DIM // 2
    cols = []
    for c in range(t.shape[1] // LANES):
        tc = t[:, c * LANES:(c + 1) * LANES]
        cols.append(tc * cos_t + pltpu.roll(tc, LANES - half, axis=1) * sin_lo + pltpu.roll(tc, half, axis=1) * sin_hi)
    return cols[0] if len(cols) == 1 else jnp.concatenate(cols, axis=1)


def _attention_kernel(sink_ref, q_ref, kc_ref, kp_ref, vc_ref, vp_ref, cos_ref, slo_ref, shi_ref,
                      cosp_ref, slop_ref, ship_ref, g_ref, o_ref):
    n = pl.program_id(1)
    blk = ATTN_BLOCK
    q = _rotary(q_ref[0], cos_ref[...], slo_ref[...], shi_ref[...]) * (HEAD_DIM ** -0.5)
    kc = _rotary(kc_ref[0], cos_ref[...], slo_ref[...], shi_ref[...])
    kp = _rotary(kp_ref[0], cosp_ref[...], slop_ref[...], ship_ref[...])
    kw = jnp.concatenate([kp, kc], axis=0)
    vw = jnp.concatenate([vp_ref[0], vc_ref[0]], axis=0)
    qb = q.astype(BF16)

    assert KV_WIDTH == LANES and 2 * HEAD_DIM == LANES and KV_HEADS == 2
    low = lax.broadcasted_iota(jnp.int32, (1, LANES), 1) < HEAD_DIM
    both_halves = lambda t: (jnp.where(low, t, pltpu.roll(t, HEAD_DIM, axis=1)).astype(BF16),
                             jnp.where(low, pltpu.roll(t, HEAD_DIM, axis=1), t).astype(BF16))
    k_dup, v_dup = both_halves(kw), both_halves(vw)

    qi = lax.broadcasted_iota(jnp.int32, (blk, 2 * blk), 0)
    ki = lax.broadcasted_iota(jnp.int32, (blk, 2 * blk), 1)
    diff = blk + qi - ki
    first_key = jnp.where(n == 0, blk, 0)
    valid = (diff >= 0) & (diff < WINDOW) & (ki >= first_key)

    head_out = []
    for h in range(ATTN_HEADS):
        kv = h // GQA_GROUP
        q_tile = qb[:, (h // 2) * LANES:(h // 2 + 1) * LANES]
        qh = jnp.where(low if h % 2 == 0 else ~low, q_tile, jnp.zeros_like(q_tile))
        s = lax.dot_general(qh, k_dup[kv], (((1,), (1,)), ((), ())), preferred_element_type=F32)
        s = jnp.where(valid, s, -jnp.inf)
        sink = sink_ref[h]
        m = jnp.maximum(jnp.max(s, axis=-1, keepdims=True), sink)
        p = jnp.exp(s - m)
        denom = jnp.sum(p, axis=-1, keepdims=True) + jnp.exp(sink - m)
        probs = (p / denom).astype(BF16)
        head_out.append(jnp.dot(probs, v_dup[kv], preferred_element_type=F32))
    o = jnp.concatenate([jnp.where(low, head_out[2 * c], head_out[2 * c + 1]) for c in range(ATTN_HEADS // 2)],
                        axis=1)
    o_ref[0] = _rms_norm(o, g_ref[...])


def _rope_tables(S):
    half = ROPE_DIM // 2
    pos = jnp.arange(S, dtype=F32)
    inv_freq = ROPE_THETA ** (-jnp.arange(0, ROPE_DIM, 2, dtype=F32) / ROPE_DIM)
    ang = pos[:, None] * inv_freq[None, :]
    cos, sin = jnp.cos(ang), jnp.sin(ang)
    ones = jnp.ones((S, HEAD_DIM - ROPE_DIM), F32)
    zeros = jnp.zeros((S, HEAD_DIM - ROPE_DIM), F32)
    zh = jnp.zeros((S, half), F32)
    per_head = lambda *parts: jnp.tile(jnp.concatenate(parts, axis=1), (1, LANES // HEAD_DIM))
    return per_head(cos, cos, ones), per_head(-sin, zh, zeros), per_head(zh, sin, zeros)


def _attention(q, k, v, sinks, norm_g, B, S):
    blk = ATTN_BLOCK
    nb = S // blk
    cos_t, sin_lo, sin_hi = _rope_tables(S)
    cur = lambda b, n: (b, n, 0)
    prev = lambda b, n: (b, jnp.maximum(n - 1, 0), 0)
    tcur = lambda b, n: (n, 0)
    tprev = lambda b, n: (jnp.maximum(n - 1, 0), 0)
    tab = lambda im: pl.BlockSpec((blk, LANES), im)
    return pl.pallas_call(
        _attention_kernel,
        grid=(B, nb),
        in_specs=[pl.BlockSpec(memory_space=pltpu.SMEM),
                  pl.BlockSpec((1, blk, ATTN_WIDTH), cur),
                  pl.BlockSpec((1, blk, KV_WIDTH), cur), pl.BlockSpec((1, blk, KV_WIDTH), prev),
                  pl.BlockSpec((1, blk, KV_WIDTH), cur), pl.BlockSpec((1, blk, KV_WIDTH), prev),
                  tab(tcur), tab(tcur), tab(tcur), tab(tprev), tab(tprev), tab(tprev),
                  pl.BlockSpec((1, ATTN_WIDTH), lambda b, n: (0, 0))],
        out_specs=pl.BlockSpec((1, blk, ATTN_WIDTH), cur),
        out_shape=jax.ShapeDtypeStruct((B, S, ATTN_WIDTH), F32),
        compiler_params=_params("parallel", "parallel"),
        name="attention",
    )(sinks, q.reshape(B, S, ATTN_WIDTH), k.reshape(B, S, KV_WIDTH), k.reshape(B, S, KV_WIDTH),
      v.reshape(B, S, KV_WIDTH), v.reshape(B, S, KV_WIDTH),
      cos_t, sin_lo, sin_hi, cos_t, sin_lo, sin_hi, norm_g.reshape(1, ATTN_WIDTH))


def _rglru_kernel(xr_ref, gin_ref, cw_ref, cb_ref, wg_ref, bg_ref, lam_ref, g_ref, o_ref, tail_ref, h_ref):
    ts, C = xr_ref.shape[1], xr_ref.shape[2]

    @pl.when(pl.program_id(1) == 0)
    def _():
        tail_ref[...] = jnp.zeros_like(tail_ref)
        h_ref[...] = jnp.zeros_like(h_ref)

    xr = xr_ref[0]
    xcat = jnp.concatenate([tail_ref[...], xr], axis=0)
    tail_ref[...] = xr[ts - SUBLANES:, :]
    xc = cb_ref[...] + cw_ref[CONV_WIDTH - 1:CONV_WIDTH, :] * xr
    for j in range(CONV_WIDTH - 1):
        back = CONV_WIDTH - 1 - j
        xc = xc + cw_ref[j:j + 1, :] * pltpu.roll(xcat, back, axis=0)[SUBLANES:, :]

    gates = jnp.dot(xc.astype(BF16), wg_ref[...], preferred_element_type=F32) + bg_ref[...]
    r = jax.nn.sigmoid(gates[:, :C])
    i = jax.nn.sigmoid(gates[:, C:])
    lam = lam_ref[...]
    softplus_neg_lam = jnp.maximum(-lam, 0.0) + jnp.log1p(jnp.exp(-jnp.abs(lam)))
    log_a = -LRU_C * r * softplus_neg_lam
    a = jnp.exp(log_a)
    b = jnp.sqrt((1.0 + jnp.exp(2.0 * log_a)) * jnp.tanh(-log_a)) * (i * xc)

    row = lax.broadcasted_iota(jnp.int32, (ts, C), 0)
    d = 1
    while d < ts:
        keep = row >= d
        b = jnp.where(keep, a * pltpu.roll(b, d, axis=0) + b, b)
        a = jnp.where(keep, a * pltpu.roll(a, d, axis=0), a)
        d *= 2
    h = b + a * h_ref[0:1, :]
    h_ref[...] = jnp.broadcast_to(h[ts - 1:ts, :], h_ref.shape)
    o_ref[0] = _rms_norm(_gelu(gin_ref[0]) * h, g_ref[...])


def _block_diag(w):
    H, a, b = w.shape
    eye = jnp.eye(H, dtype=w.dtype)
    return (w[:, :, None, :] * eye[:, None, :, None]).reshape(H * a, H * b)


def _rglru(xr, gate_in, conv_w, conv_b, gate_a_w, gate_a_b, gate_x_w, gate_x_b, lru_lambda, norm_g, B, S, ts=256):
    C = xr.shape[-1]
    wg = jnp.concatenate([_block_diag(gate_a_w), _block_diag(gate_x_w)], axis=1).astype(BF16)
    bg = jnp.concatenate([gate_a_b, gate_x_b]).reshape(1, 2 * C)
    cur = lambda b, n: (b, n, 0)
    fixed = lambda b, n: (0, 0)
    return pl.pallas_call(
        _rglru_kernel,
        grid=(B, S // ts),
        in_specs=[pl.BlockSpec((1, ts, C), cur), pl.BlockSpec((1, ts, C), cur),
                  pl.BlockSpec((CONV_WIDTH, C), fixed), pl.BlockSpec((1, C), fixed),
                  pl.BlockSpec((C, 2 * C), fixed), pl.BlockSpec((1, 2 * C), fixed),
                  pl.BlockSpec((1, C), fixed), pl.BlockSpec((1, C), fixed)],
        out_specs=pl.BlockSpec((1, ts, C), cur),
        out_shape=jax.ShapeDtypeStruct((B, S, C), F32),
        scratch_shapes=[pltpu.VMEM((SUBLANES, C), F32), pltpu.VMEM((SUBLANES, C), F32)],
        compiler_params=_params("parallel", "arbitrary"),
        name="rglru",
    )(xr.reshape(B, S, C), gate_in.reshape(B, S, C), conv_w, conv_b.reshape(1, C), wg, bg,
      lru_lambda.reshape(1, C), norm_g.reshape(1, C))


def _post_mix_kernel(x_ref, ma_ref, mr_ref, woa_ref, wor_ref, bo_ref, g1_ref, b1_ref, wq_ref, k1_ref, k2_ref,
                     h_ref, s_ref):
    y = (jnp.dot(ma_ref[...].astype(BF16), woa_ref[...], preferred_element_type=F32)
         + jnp.dot(mr_ref[...].astype(BF16), wor_ref[...], preferred_element_type=F32) + bo_ref[...])
    h = _layer_norm(ALPHA * x_ref[...] + y, g1_ref[...], b1_ref[...])
    h_ref[...] = h
    q = jnp.dot(h.astype(BF16), wq_ref[...], preferred_element_type=F32).astype(BF16)
    nt = (((1,), (1,)), ((), ()))
    for hd in range(PEER_HEADS):
        for half, k_ref in enumerate((k1_ref, k2_ref)):
            c = (hd * 2 + half) * PEER_HALF_DIM
            for blk in range(s_ref.shape[1]):
                qb = q[blk * LANES:(blk + 1) * LANES, c:c + PEER_HALF_DIM]
                s_ref[hd * 2 + half, blk] = lax.dot_general(k_ref[...], qb, nt, preferred_element_type=F32)


def _post_mix(x2, mixed_attn, mixed_rnn, w_out, b_out, ln1_g, ln1_b, peer_w_q, keys_1, keys_2, tm=512):
    T, D = x2.shape
    wa, wr = mixed_attn.shape[1], mixed_rnn.shape[1]
    nq = peer_w_q.shape[1]
    row = lambda i: (i, 0)
    fixed = lambda i: (0, 0)
    full = lambda a: pl.BlockSpec(a.shape, fixed)
    wo = w_out.astype(BF16)
    woa, wor = wo[:wa], wo[wa:]
    wq = peer_w_q.astype(BF16)
    k1, k2 = keys_1.astype(BF16), keys_2.astype(BF16)
    vec = lambda a: a.reshape(1, -1)
    return pl.pallas_call(
        _post_mix_kernel,
        grid=(T // tm,),
        in_specs=[pl.BlockSpec((tm, D), row), pl.BlockSpec((tm, wa), row), pl.BlockSpec((tm, wr), row),
                  full(woa), full(wor), pl.BlockSpec((1, D), fixed), pl.BlockSpec((1, D), fixed),
                  pl.BlockSpec((1, D), fixed), full(wq), full(k1), full(k2)],
        out_specs=[pl.BlockSpec((tm, D), row),
                   pl.BlockSpec((2 * PEER_HEADS, tm // LANES, PEER_N_KEYS, LANES), lambda i: (0, i, 0, 0))],
        out_shape=[jax.ShapeDtypeStruct((T, D), F32),
                   jax.ShapeDtypeStruct((2 * PEER_HEADS, T // LANES, PEER_N_KEYS, LANES), F32)],
        compiler_params=_params("parallel"),
        name="post_mix",
    )(x2, mixed_attn, mixed_rnn, woa, wor, vec(b_out), vec(ln1_g), vec(ln1_b), wq, k1, k2)


def _before(a, b):
    return (a[0] > b[0]) | ((a[0] == b[0]) & (a[1] < b[1]))


def _pick(cond, a, b):
    return tuple(jnp.where(cond, x, y) for x, y in zip(a, b))


def _compare_exchange(xs, i, j):
    swap = _before(xs[j], xs[i])
    xs[i], xs[j] = _pick(swap, xs[j], xs[i]), _pick(swap, xs[i], xs[j])


def _batcher_pairs(n):
    pairs, p = [], 1
    while p < n:
        k = p
        while k >= 1:
            for j in range(k % p, n - k, 2 * k):
                for i in range(min(k, n - j - k)):
                    if (i + j) // (2 * p) == (i + j + k) // (2 * p):
                        pairs.append((i + j, i + j + k))
            k //= 2
        p *= 2
    return pairs


def _sort(xs):
    xs = list(xs)
    for i, j in _batcher_pairs(len(xs)):
        _compare_exchange(xs, i, j)
    return xs


def _best_of_two_sorted(a, b, sort_result=True):
    n = len(a)
    c = [_pick(_before(b[n - 1 - i], a[i]), b[n - 1 - i], a[i]) for i in range(n)]
    if sort_result:
        d = n // 2
        while d >= 1:
            for i in range(n):
                if not i & d:
                    _compare_exchange(c, i, i + d)
            d //= 2
    return c


def _peer_topk_kernel(s_ref, e_ref, g_ref, km, sv, si):
    K = PEER_TOPK
    n_groups = PEER_N_KEYS // K
    tile = lambda fill: jnp.full((SUBLANES, LANES), fill, jnp.int32)

    def head(hd, carry):
        for half in range(2):
            for key in range(PEER_N_KEYS):
                km[key] = s_ref[2 * hd + half, :, key, :]

            def sort_group(g, c, half=half):
                xs = _sort([(km[g * K + r], tile(g * K + r)) for r in range(K)])
                for r in range(K):
                    sv[half, g * K + r], si[half, g * K + r] = xs[r]
                return c

            def merge(m, c, half=half):
                read = lambda slot: [(sv[half, slot * K + r], si[half, slot * K + r]) for r in range(K)]
                xs = _best_of_two_sorted(read(2 * m), read(2 * m + 1))
                for r in range(K):
                    sv[half, (n_groups + m) * K + r], si[half, (n_groups + m) * K + r] = xs[r]
                return c

            lax.fori_loop(0, n_groups, sort_group, 0)
            lax.fori_loop(0, n_groups - 1, merge, 0)

        last = (2 * n_groups - 2) * K
        v1 = [(sv[0, last + r], si[0, last + r]) for r in range(K)]
        v2 = [(sv[1, last + r], si[1, last + r]) for r in range(K)]
        def cand(i, j):
            return (v1[i][0] + v2[j][0], tile(i * K + j), v1[i][1] * PEER_N_KEYS + v2[j][1])
        rows = [[cand(i, j) for j in range(K // (i + 1))] for i in range(K)]
        assert K == 16
        g1 = _sort(rows[1] + rows[2] + rows[4])
        g2 = _sort(rows[3] + rows[5] + rows[6] + rows[7] + [r[0] for r in rows[8:14]])
        top = _best_of_two_sorted(_best_of_two_sorted(rows[0], g1), g2)
        top[K - 1] = _pick(_before(rows[14][0], top[K - 1]), rows[14][0], top[K - 1])
        top[K - 2] = _pick(_before(rows[15][0], top[K - 2]), rows[15][0], top[K - 2])
        m = top[0][0]
        for t in top[1:]:
            m = jnp.maximum(m, t[0])
        p = [jnp.exp(t[0] - m) for t in top]
        denom = p[0]
        for x in p[1:]:
            denom = denom + x
        for r in range(K):
            e_ref[hd * K + r] = top[r][2]
            g_ref[hd * K + r] = p[r] / denom
        return carry

    lax.fori_loop(0, PEER_HEADS, head, 0)


def _peer_topk(scores):
    n_sets, n_blocks, n_keys, _ = scores.shape
    T = n_blocks * LANES
    E = PEER_HEADS * PEER_TOPK
    out_map = lambda i: (0, i, 0)
    e3, g3 = pl.pallas_call(
        _peer_topk_kernel,
        grid=(n_blocks // SUBLANES,),
        in_specs=[pl.BlockSpec((n_sets, SUBLANES, n_keys, LANES), lambda i: (0, i, 0, 0))],
        out_specs=[pl.BlockSpec((E, SUBLANES, LANES), out_map), pl.BlockSpec((E, SUBLANES, LANES), out_map)],
        out_shape=[jax.ShapeDtypeStruct((E, n_blocks, LANES), jnp.int32),
                   jax.ShapeDtypeStruct((E, n_blocks, LANES), F32)],
        scratch_shapes=[pltpu.VMEM((n_keys, SUBLANES, LANES), F32),
                        pltpu.VMEM((2, (2 * n_keys // PEER_TOPK - 1) * PEER_TOPK, SUBLANES, LANES), F32),
                        pltpu.VMEM((2, (2 * n_keys // PEER_TOPK - 1) * PEER_TOPK, SUBLANES, LANES), jnp.int32)],
        compiler_params=_params("parallel"),
        name="peer_topk",
    )(scores)
    return e3.reshape(E, T).T, g3.reshape(E, T).T


def _sum_over_sublanes(p, lo4, lo2, lo1):
    z = [jnp.where(lo4, p[k], p[k + 4]) + pltpu.roll(jnp.where(lo4, p[k + 4], p[k]), 4, axis=0) for k in range(4)]
    y = [jnp.where(lo2, z[k] + pltpu.roll(z[k], SUBLANES - 2, axis=0), z[k + 2] + pltpu.roll(z[k + 2], 2, axis=0))
         for k in range(2)]
    return jnp.where(lo1, y[0] + pltpu.roll(y[0], SUBLANES - 1, axis=0), y[1] + pltpu.roll(y[1], 1, axis=0))


def _peer_apply_kernel(ex_ref, exn_ref, g_ref, h2d_ref, ln_g_ref, ln_b_ref, uv_hbm, o_ref, buf, pbuf0, pbuf1, wbuf0, wbuf1,
                       h_ref, zbuf, sem):
    i = pl.program_id(0)
    n = pl.num_programs(0)
    tk, E = ex_ref.shape
    slot = lax.rem(i, 2)
    nslot = 1 - slot
    pbuf, wbuf = (pbuf0, pbuf1), (wbuf0, wbuf1)
    for c in range(SUBLANES):
        h_ref[:, c, :] = h2d_ref[:, c * LANES:(c + 1) * LANES]

    def row_copy(e, s, t, j):
        return pltpu.make_async_copy(uv_hbm.at[e], buf.at[s * tk + t, j], sem.at[s, t])

    def start_rows(idx_ref, s, t, j0, j1):
        for j in range(j0, j1):
            row_copy(idx_ref[t, j], s, t, j).start(priority=j % 2)

    def start_token(idx_ref, s, t):
        start_rows(idx_ref, s, t, 0, E)

    def wait_token(s, t):
        pltpu.make_async_copy(buf.at[s * tk + t], buf.at[s * tk + t], sem.at[s, t]).wait()

    @pl.when(i == 0)
    def _():
        def body(t, carry):
            start_token(ex_ref, slot, t)
            return carry
        lax.fori_loop(0, tk, body, 0)

    sub = lax.broadcasted_iota(jnp.int32, (SUBLANES, LANES), 0)
    lo4, lo2, lo1 = sub < 4, (sub & 2) == 0, (sub & 1) == 0
    eye = lax.broadcasted_iota(jnp.int32, (E, LANES), 0) == lax.broadcasted_iota(jnp.int32, (E, LANES), 1)
    n_acc = 4

    groups = E // SUBLANES
    per_chunk = E // groups

    def dots_group(t, g):
        h = h_ref[t]
        base = slot * tk + t
        return _sum_over_sublanes([buf[base, g * SUBLANES + k, 0:SUBLANES, :] * h for k in range(SUBLANES)],
                                  lo4, lo2, lo1)

    def weights(t, par):
        act = jnp.sum(pbuf[par][...], axis=1, keepdims=True)
        gate = jnp.sum(jnp.where(eye, g_ref[pl.ds(t, 1), :], 0.0), axis=1, keepdims=True)
        return jnp.broadcast_to(gate * _gelu(act), (E, LANES))

    def mix_group(t, par, g, acc):
        base = slot * tk + t
        for e in range(g * SUBLANES, (g + 1) * SUBLANES):
            w = jnp.broadcast_to(wbuf[par][e:e + 1, :], (SUBLANES, LANES))
            term = w * buf[base, e, SUBLANES:2 * SUBLANES, :]
            acc[e % n_acc] = term if acc[e % n_acc] is None else acc[e % n_acc] + term

    def stage(t, par, with_dots, with_weights, with_mix=True):
        if with_dots:
            wait_token(slot, t + 2)
        w_next = weights(t + 1, 1 - par) if with_weights else None
        acc, pending = [None] * n_acc, None
        for g in range(groups):
            if with_mix:
                start_rows(exn_ref, nslot, t, g * per_chunk, (g + 1) * per_chunk)
            if with_dots:
                p = dots_group(t + 2, g)
                if pending is not None:
                    pbuf[par][(g - 1) * SUBLANES:g * SUBLANES, :] = pending
                pending = p
            if with_mix:
                mix_group(t, par, g, acc)
        if with_dots:
            pbuf[par][(groups - 1) * SUBLANES:, :] = pending
        if with_mix:
            zbuf[t] = ALPHA * h_ref[t] + ((acc[0] + acc[1]) + (acc[2] + acc[3]))
        if with_weights:
            wbuf[1 - par][...] = w_next

    stage(-2, 0, True, False, with_mix=False)
    stage(-1, 1, True, True, with_mix=False)

    def token_pair(tt, carry):
        t = 2 * tt
        stage(t, 0, True, True)
        stage(t + 1, 1, True, True)
        return carry

    assert tk % 2 == 0 and tk >= 4
    lax.fori_loop(0, (tk - 2) // 2, token_pair, 0)

    stage(tk - 2, 0, False, True)
    stage(tk - 1, 1, False, False)


    z = zbuf[...]
    total = lambda a: jnp.sum(jnp.sum(a, axis=2, keepdims=True), axis=1, keepdims=True)
    inv_d = 1.0 / (SUBLANES * LANES)
    zc = z - total(z) * inv_d
    var = total(zc * zc) * inv_d
    zbuf[...] = zc * lax.rsqrt(var + EPS) * ln_g_ref[...] + ln_b_ref[...]
    for c in range(SUBLANES):
        o_ref[:, c * LANES:(c + 1) * LANES] = zbuf[:, c, :]

    @pl.when(i == n - 1)
    def _():
        def body(t, carry):
            wait_token(nslot, t)
            return carry
        lax.fori_loop(0, tk, body, 0)


def _peer_apply(h1, experts, gates, peer_u, peer_v, ln2_g, ln2_b, tk=16):
    T, D = h1.shape
    E = experts.shape[1]
    assert D == SUBLANES * LANES and E == LANES
    n = T // tk
    tile = lambda a: a.reshape(-1, SUBLANES, LANES)
    uv = jnp.concatenate([tile(peer_u), tile(peer_v)], axis=1)
    row = lambda i: (i, 0)
    nxt = lambda i: (jnp.minimum(i + 1, n - 1), 0)
    fixed = lambda i: (0, 0)
    return pl.pallas_call(
        _peer_apply_kernel,
        grid=(n,),
        in_specs=[pl.BlockSpec((tk, E), row, memory_space=pltpu.SMEM),
                  pl.BlockSpec((tk, E), nxt, memory_space=pltpu.SMEM),
                  pl.BlockSpec((tk, E), row), pl.BlockSpec((tk, D), row),
                  pl.BlockSpec((SUBLANES, LANES), fixed), pl.BlockSpec((SUBLANES, LANES), fixed),
                  pl.BlockSpec(memory_space=pl.ANY)],
        out_specs=pl.BlockSpec((tk, D), row),
        out_shape=jax.ShapeDtypeStruct((T, D), F32),
        scratch_shapes=[pltpu.VMEM((2 * tk, E, 2 * SUBLANES, LANES), F32)]
                       + [pltpu.VMEM((E, LANES), F32)] * 4
                       + [pltpu.VMEM((tk, SUBLANES, LANES), F32)] * 2 + [pltpu.SemaphoreType.DMA((2, tk))],
        compiler_params=_params("arbitrary"),
        name="peer_apply",
    )(experts, experts, gates, h1, ln2_g.reshape(SUBLANES, LANES), ln2_b.reshape(SUBLANES, LANES), uv)


def kernel(x, w_in, b_in, attn_sinks, conv_w, conv_b, gate_a_w, gate_a_b, gate_x_w, gate_x_b, lru_lambda, norm_attn_g, norm_rnn_g, w_out, b_out, ln1_g, ln1_b, peer_w_q, peer_keys_1, peer_keys_2, peer_u, peer_v, ln2_g, ln2_b):
    B, S, D = x.shape
    x2 = x.reshape(B * S, D)
    rnn_width = conv_w.shape[1]
    q, k, v, xr, gate_in = _in_proj(x2, w_in, b_in, rnn_width)
    mixed_attn = _attention(q, k, v, attn_sinks, norm_attn_g, B, S).reshape(B * S, ATTN_WIDTH)
    mixed_rnn = _rglru(xr, gate_in, conv_w, conv_b, gate_a_w, gate_a_b, gate_x_w, gate_x_b, lru_lambda,
                       norm_rnn_g, B, S).reshape(B * S, rnn_width)
    h1, scores_t = _post_mix(x2, mixed_attn, mixed_rnn, w_out, b_out, ln1_g, ln1_b, peer_w_q,
                             peer_keys_1, peer_keys_2)
    experts, gates = _peer_topk(scores_t)
    out = _peer_apply(h1, experts, gates, peer_u, peer_v, ln2_g, ln2_b)
    return out.reshape(B, S, D)
```

```python
import functools
import math

import jax
import jax.numpy as jnp
from jax import lax
from jax.experimental import pallas as pl
from jax.experimental.pallas import tpu as pltpu

F32 = jnp.float32
BF16 = jnp.bfloat16

ATTN_HEADS = 8
KV_HEADS = 2
HEAD_DIM = 64
GQA_GROUP = ATTN_HEADS // KV_HEADS
ATTN_WIDTH = ATTN_HEADS * HEAD_DIM
KV_WIDTH = KV_HEADS * HEAD_DIM
WINDOW = 128
ATTN_BLOCK = 128
ROPE_DIM = HEAD_DIM // 4
ROPE_THETA = 500000.0
RNN_HEADS = 8
CONV_WIDTH = 4
LRU_C = 8.0
PEER_HEADS = 8
PEER_N_KEYS = 128
PEER_HALF_DIM = 128
PEER_TOPK = 16
DEPTH = 1
ALPHA = (2.0 * DEPTH) ** 0.25
EPS = 1e-5

LANES = 128
SUBLANES = 8
VMEM_LIMIT_BYTES = 56 * 1024 * 1024


def _params(*semantics):
    return pltpu.CompilerParams(dimension_semantics=semantics, vmem_limit_bytes=VMEM_LIMIT_BYTES)


def _gelu(x):
    c = math.sqrt(2.0 / math.pi)
    return 0.5 * x * (1.0 + jnp.tanh(c * (x + 0.044715 * (x * x * x))))


def _layer_norm(x, g, b):
    mu = jnp.mean(x, axis=-1, keepdims=True)
    xc = x - mu
    var = jnp.mean(xc * xc, axis=-1, keepdims=True)
    return xc * lax.rsqrt(var + EPS) * g + b


def _rms_norm(x, g):
    return x * lax.rsqrt(jnp.mean(x * x, axis=-1, keepdims=True) + EPS) * g


def _in_proj_kernel(x_ref, w_ref, b_ref, q_ref, k_ref, v_ref, xr_ref, g_ref):
    z = jnp.dot(x_ref[...].astype(BF16), w_ref[...], preferred_element_type=F32) + b_ref[...]
    c0 = q_ref.shape[1]
    c1 = c0 + k_ref.shape[1]
    c2 = c1 + v_ref.shape[1]
    c3 = c2 + xr_ref.shape[1]
    q_ref[...] = z[:, :c0]
    k_ref[...] = z[:, c0:c1]
    v_ref[...] = z[:, c1:c2]
    xr_ref[...] = z[:, c2:c3]
    g_ref[...] = z[:, c3:]


def _in_proj(x2, w_in, b_in, rnn_width, tm=512):
    T, D = x2.shape
    n_in = w_in.shape[1]
    widths = (ATTN_WIDTH, KV_WIDTH, KV_WIDTH, rnn_width, rnn_width)
    row = lambda i: (i, 0)
    fixed = lambda i: (0, 0)
    return pl.pallas_call(
        _in_proj_kernel,
        grid=(T // tm,),
        in_specs=[pl.BlockSpec((tm, D), row), pl.BlockSpec((D, n_in), fixed), pl.BlockSpec((1, n_in), fixed)],
        out_specs=[pl.BlockSpec((tm, w), row) for w in widths],
        out_shape=[jax.ShapeDtypeStruct((T, w), F32) for w in widths],
        compiler_params=_params("parallel"),
        name="in_proj",
    )(x2, w_in.astype(BF16), b_in.reshape(1, n_in))


def _rotary(t, cos_t, sin_lo, sin_hi):
    half = ROPE_DIM // 2
    cols = []
    for c in range(t.shape[1] // LANES):
        tc = t[:, c * LANES:(c + 1) * LANES]
        cols.append(tc * cos_t + pltpu.roll(tc, LANES - half, axis=1) * sin_lo + pltpu.roll(tc, half, axis=1) * sin_hi)
    return cols[0] if len(cols) == 1 else jnp.concatenate(cols, axis=1)


def _attention_kernel(sink_ref, q_ref, kc_ref, kp_ref, vc_ref, vp_ref, cos_ref, slo_ref, shi_ref,
                      cosp_ref, slop_ref, ship_ref, g_ref, o_ref):
    n = pl.program_id(1)
    blk = ATTN_BLOCK
    q = _rotary(q_ref[0], cos_ref[...], slo_ref[...], shi_ref[...]) * (HEAD_DIM ** -0.5)
    kc = _rotary(kc_ref[0], cos_ref[...], slo_ref[...], shi_ref[...])
    kp = _rotary(kp_ref[0], cosp_ref[...], slop_ref[...], ship_ref[...])
    kw = jnp.concatenate([kp, kc], axis=0)
    vw = jnp.concatenate([vp_ref[0], vc_ref[0]], axis=0)
    qb = q.astype(BF16)

    assert KV_WIDTH == LANES and 2 * HEAD_DIM == LANES and KV_HEADS == 2
    low = lax.broadcasted_iota(jnp.int32, (1, LANES), 1) < HEAD_DIM
    both_halves = lambda t: (jnp.where(low, t, pltpu.roll(t, HEAD_DIM, axis=1)).astype(BF16),
                             jnp.where(low, pltpu.roll(t, HEAD_DIM, axis=1), t).astype(BF16))
    k_dup, v_dup = both_halves(kw), both_halves(vw)

    qi = lax.broadcasted_iota(jnp.int32, (blk, 2 * blk), 0)
    ki = lax.broadcasted_iota(jnp.int32, (blk, 2 * blk), 1)
    diff = blk + qi - ki
    first_key = jnp.where(n == 0, blk, 0)
    valid = (diff >= 0) & (diff < WINDOW) & (ki >= first_key)

    head_out = []
    for h in range(ATTN_HEADS):
        kv = h // GQA_GROUP
        q_tile = qb[:, (h // 2) * LANES:(h // 2 + 1) * LANES]
        qh = jnp.where(low if h % 2 == 0 else ~low, q_tile, jnp.zeros_like(q_tile))
        s = lax.dot_general(qh, k_dup[kv], (((1,), (1,)), ((), ())), preferred_element_type=F32)
        s = jnp.where(valid, s, -jnp.inf)
        sink = sink_ref[h]
        m = jnp.maximum(jnp.max(s, axis=-1, keepdims=True), sink)
        p = jnp.exp(s - m)
        denom = jnp.sum(p, axis=-1, keepdims=True) + jnp.exp(sink - m)
        probs = (p / denom).astype(BF16)
        head_out.append(jnp.dot(probs, v_dup[kv], preferred_element_type=F32))
    o = jnp.concatenate([jnp.where(low, head_out[2 * c], head_out[2 * c + 1]) for c in range(ATTN_HEADS // 2)],
                        axis=1)
    o_ref[0] = _rms_norm(o, g_ref[...])


def _rope_tables(S):
    half = ROPE_DIM // 2
    pos = jnp.arange(S, dtype=F32)
    inv_freq = ROPE_THETA ** (-jnp.arange(0, ROPE_DIM, 2, dtype=F32) / ROPE_DIM)
    ang = pos[:, None] * inv_freq[None, :]
    cos, sin = jnp.cos(ang), jnp.sin(ang)
    ones = jnp.ones((S, HEAD_DIM - ROPE_DIM), F32)
    zeros = jnp.zeros((S, HEAD_DIM - ROPE_DIM), F32)
    zh = jnp.zeros((S, half), F32)
    per_head = lambda *parts: jnp.tile(jnp.concatenate(parts, axis=1), (1, LANES // HEAD_DIM))
    return per_head(cos, cos, ones), per_head(-sin, zh, zeros), per_head(zh, sin, zeros)


def _attention(q, k, v, sinks, norm_g, B, S):
    blk = ATTN_BLOCK
    nb = S // blk
    cos_t, sin_lo, sin_hi = _rope_tables(S)
    cur = lambda b, n: (b, n, 0)
    prev = lambda b, n: (b, jnp.maximum(n - 1, 0), 0)
    tcur = lambda b, n: (n, 0)
    tprev = lambda b, n: (jnp.maximum(n - 1, 0), 0)
    tab = lambda im: pl.BlockSpec((blk, LANES), im)
    return pl.pallas_call(
        _attention_kernel,
        grid=(B, nb),
        in_specs=[pl.BlockSpec(memory_space=pltpu.SMEM),
                  pl.BlockSpec((1, blk, ATTN_WIDTH), cur),
                  pl.BlockSpec((1, blk, KV_WIDTH), cur), pl.BlockSpec((1, blk, KV_WIDTH), prev),
                  pl.BlockSpec((1, blk, KV_WIDTH), cur), pl.BlockSpec((1, blk, KV_WIDTH), prev),
                  tab(tcur), tab(tcur), tab(tcur), tab(tprev), tab(tprev), tab(tprev),
                  pl.BlockSpec((1, ATTN_WIDTH), lambda b, n: (0, 0))],
        out_specs=pl.BlockSpec((1, blk, ATTN_WIDTH), cur),
        out_shape=jax.ShapeDtypeStruct((B, S, ATTN_WIDTH), F32),
        compiler_params=_params("parallel", "parallel"),
        name="attention",
    )(sinks, q.reshape(B, S, ATTN_WIDTH), k.reshape(B, S, KV_WIDTH), k.reshape(B, S, KV_WIDTH),
      v.reshape(B, S, KV_WIDTH), v.reshape(B, S, KV_WIDTH),
      cos_t, sin_lo, sin_hi, cos_t, sin_lo, sin_hi, norm_g.reshape(1, ATTN_WIDTH))


def _rglru_kernel(xr_ref, gin_ref, cw_ref, cb_ref, wg_ref, bg_ref, lam_ref, g_ref, o_ref, tail_ref, h_ref):
    ts, C = xr_ref.shape[1], xr_ref.shape[2]

    @pl.when(pl.program_id(1) == 0)
    def _():
        tail_ref[...] = jnp.zeros_like(tail_ref)
        h_ref[...] = jnp.zeros_like(h_ref)

    xr = xr_ref[0]
    xcat = jnp.concatenate([tail_ref[...], xr], axis=0)
    tail_ref[...] = xr[ts - SUBLANES:, :]
    xc = cb_ref[...] + cw_ref[CONV_WIDTH - 1:CONV_WIDTH, :] * xr
    for j in range(CONV_WIDTH - 1):
        back = CONV_WIDTH - 1 - j
        xc = xc + cw_ref[j:j + 1, :] * pltpu.roll(xcat, back, axis=0)[SUBLANES:, :]

    gates = jnp.dot(xc.astype(BF16), wg_ref[...], preferred_element_type=F32) + bg_ref[...]
    r = jax.nn.sigmoid(gates[:, :C])
    i = jax.nn.sigmoid(gates[:, C:])
    lam = lam_ref[...]
    softplus_neg_lam = jnp.maximum(-lam, 0.0) + jnp.log1p(jnp.exp(-jnp.abs(lam)))
    log_a = -LRU_C * r * softplus_neg_lam
    a = jnp.exp(log_a)
    b = jnp.sqrt((1.0 + jnp.exp(2.0 * log_a)) * jnp.tanh(-log_a)) * (i * xc)

    row = lax.broadcasted_iota(jnp.int32, (ts, C), 0)
    d = 1
    while d < ts:
        keep = row >= d
        b = jnp.where(keep, a * pltpu.roll(b, d, axis=0) + b, b)
        a = jnp.where(keep, a * pltpu.roll(a, d, axis=0), a)
        d *= 2
    h = b + a * h_ref[0:1, :]
    h_ref[...] = jnp.broadcast_to(h[ts - 1:ts, :], h_ref.shape)
    o_ref[0] = _rms_norm(_gelu(gin_ref[0]) * h, g_ref[...])


def _block_diag(w):
    H, a, b = w.shape
    eye = jnp.eye(H, dtype=w.dtype)
    return (w[:, :, None, :] * eye[:, None, :, None]).reshape(H * a, H * b)


def _rglru(xr, gate_in, conv_w, conv_b, gate_a_w, gate_a_b, gate_x_w, gate_x_b, lru_lambda, norm_g, B, S, ts=256):
    C = xr.shape[-1]
    wg = jnp.concatenate([_block_diag(gate_a_w), _block_diag(gate_x_w)], axis=1).astype(BF16)
    bg = jnp.concatenate([gate_a_b, gate_x_b]).reshape(1, 2 * C)
    cur = lambda b, n: (b, n, 0)
    fixed = lambda b, n: (0, 0)
    return pl.pallas_call(
        _rglru_kernel,
        grid=(B, S // ts),
        in_specs=[pl.BlockSpec((1, ts, C), cur), pl.BlockSpec((1, ts, C), cur),
                  pl.BlockSpec((CONV_WIDTH, C), fixed), pl.BlockSpec((1, C), fixed),
                  pl.BlockSpec((C, 2 * C), fixed), pl.BlockSpec((1, 2 * C), fixed),
                  pl.BlockSpec((1, C), fixed), pl.BlockSpec((1, C), fixed)],
        out_specs=pl.BlockSpec((1, ts, C), cur),
        out_shape=jax.ShapeDtypeStruct((B, S, C), F32),
        scratch_shapes=[pltpu.VMEM((SUBLANES, C), F32), pltpu.VMEM((SUBLANES, C), F32)],
        compiler_params=_params("parallel", "arbitrary"),
        name="rglru",
    )(xr.reshape(B, S, C), gate_in.reshape(B, S, C), conv_w, conv_b.reshape(1, C), wg, bg,
      lru_lambda.reshape(1, C), norm_g.reshape(1, C))


def _post_mix_kernel(x_ref, ma_ref, mr_ref, woa_ref, wor_ref, bo_ref, g1_ref, b1_ref, wq_ref, k1_ref, k2_ref,
                     h_ref, s_ref):
    y = (jnp.dot(ma_ref[...].astype(BF16), woa_ref[...], preferred_element_type=F32)
         + jnp.dot(mr_ref[...].astype(BF16), wor_ref[...], preferred_element_type=F32) + bo_ref[...])
    h = _layer_norm(ALPHA * x_ref[...] + y, g1_ref[...], b1_ref[...])
    h_ref[...] = h
    q = jnp.dot(h.astype(BF16), wq_ref[...], preferred_element_type=F32).astype(BF16)
    nt = (((1,), (1,)), ((), ()))
    for hd in range(PEER_HEADS):
        for half, k_ref in enumerate((k1_ref, k2_ref)):
            c = (hd * 2 + half) * PEER_HALF_DIM
            for blk in range(s_ref.shape[1]):
                qb = q[blk * LANES:(blk + 1) * LANES, c:c + PEER_HALF_DIM]
                s_ref[hd * 2 + half, blk] = lax.dot_general(k_ref[...], qb, nt, preferred_element_type=F32)


def _post_mix(x2, mixed_attn, mixed_rnn, w_out, b_out, ln1_g, ln1_b, peer_w_q, keys_1, keys_2, tm=512):
    T, D = x2.shape
    wa, wr = mixed_attn.shape[1], mixed_rnn.shape[1]
    nq = peer_w_q.shape[1]
    row = lambda i: (i, 0)
    fixed = lambda i: (0, 0)
    full = lambda a: pl.BlockSpec(a.shape, fixed)
    wo = w_out.astype(BF16)
    woa, wor = wo[:wa], wo[wa:]
    wq = peer_w_q.astype(BF16)
    k1, k2 = keys_1.astype(BF16), keys_2.astype(BF16)
    vec = lambda a: a.reshape(1, -1)
    return pl.pallas_call(
        _post_mix_kernel,
        grid=(T // tm,),
        in_specs=[pl.BlockSpec((tm, D), row), pl.BlockSpec((tm, wa), row), pl.BlockSpec((tm, wr), row),
                  full(woa), full(wor), pl.BlockSpec((1, D), fixed), pl.BlockSpec((1, D), fixed),
                  pl.BlockSpec((1, D), fixed), full(wq), full(k1), full(k2)],
        out_specs=[pl.BlockSpec((tm, D), row),
                   pl.BlockSpec((2 * PEER_HEADS, tm // LANES, PEER_N_KEYS, LANES), lambda i: (0, i, 0, 0))],
        out_shape=[jax.ShapeDtypeStruct((T, D), F32),
                   jax.ShapeDtypeStruct((2 * PEER_HEADS, T // LANES, PEER_N_KEYS, LANES), F32)],
        compiler_params=_params("parallel"),
        name="post_mix",
    )(x2, mixed_attn, mixed_rnn, woa, wor, vec(b_out), vec(ln1_g), vec(ln1_b), wq, k1, k2)


def _before(a, b):
    return (a[0] > b[0]) | ((a[0] == b[0]) & (a[1] < b[1]))


def _pick(cond, a, b):
    return tuple(jnp.where(cond, x, y) for x, y in zip(a, b))


def _compare_exchange(xs, i, j):
    swap = _before(xs[j], xs[i])
    xs[i], xs[j] = _pick(swap, xs[j], xs[i]), _pick(swap, xs[i], xs[j])


def _batcher_pairs(n):
    pairs, p = [], 1
    while p < n:
        k = p
        while k >= 1:
            for j in range(k % p, n - k, 2 * k):
                for i in range(min(k, n - j - k)):
                    if (i + j) // (2 * p) == (i + j + k) // (2 * p):
                        pairs.append((i + j, i + j + k))
            k //= 2
        p *= 2
    return pairs


def _sort(xs):
    xs = list(xs)
    for i, j in _batcher_pairs(len(xs)):
        _compare_exchange(xs, i, j)
    return xs


def _best_of_two_sorted(a, b, sort_result=True):
    n = len(a)
    c = [_pick(_before(b[n - 1 - i], a[i]), b[n - 1 - i], a[i]) for i in range(n)]
    if sort_result:
        d = n // 2
        while d >= 1:
            for i in range(n):
                if not i & d:
                    _compare_exchange(c, i, i + d)
            d //= 2
    return c


def _peer_topk_kernel(s_ref, e_ref, g_ref, km, sv, si):
    K = PEER_TOPK
    n_groups = PEER_N_KEYS // K
    tile = lambda fill: jnp.full((SUBLANES, LANES), fill, jnp.int32)

    def head(hd, carry):
        for half in range(2):
            for key in range(PEER_N_KEYS):
                km[key] = s_ref[2 * hd + half, :, key, :]

            def sort_group(g, c, half=half):
                xs = _sort([(km[g * K + r], tile(g * K + r)) for r in range(K)])
                for r in range(K):
                    sv[half, g * K + r], si[half, g * K + r] = xs[r]
                return c

            def merge(m, c, half=half):
                read = lambda slot: [(sv[half, slot * K + r], si[half, slot * K + r]) for r in range(K)]
                xs = _best_of_two_sorted(read(2 * m), read(2 * m + 1))
                for r in range(K):
                    sv[half, (n_groups + m) * K + r], si[half, (n_groups + m) * K + r] = xs[r]
                return c

            lax.fori_loop(0, n_groups, sort_group, 0)
            lax.fori_loop(0, n_groups - 1, merge, 0)

        last = (2 * n_groups - 2) * K
        v1 = [(sv[0, last + r], si[0, last + r]) for r in range(K)]
        v2 = [(sv[1, last + r], si[1, last + r]) for r in range(K)]
        def cand(i, j):
            return (v1[i][0] + v2[j][0], tile(i * K + j), v1[i][1] * PEER_N_KEYS + v2[j][1])
        rows = [[cand(i, j) for j in range(K // (i + 1))] for i in range(K)]
        assert K == 16
        g1 = _sort(rows[1] + rows[2] + rows[4])
        g2 = _sort(rows[3] + rows[5] + rows[6] + rows[7] + [r[0] for r in rows[8:14]])
        top = _best_of_two_sorted(_best_of_two_sorted(rows[0], g1), g2)
        top[K - 1] = _pick(_before(rows[14][0], top[K - 1]), rows[14][0], top[K - 1])
        top[K - 2] = _pick(_before(rows[15][0], top[K - 2]), rows[15][0], top[K - 2])
        m = top[0][0]
        for t in top[1:]:
            m = jnp.maximum(m, t[0])
        p = [jnp.exp(t[0] - m) for t in top]
        denom = p[0]
        for x in p[1:]:
            denom = denom + x
        for r in range(K):
            e_ref[hd * K + r] = top[r][2]
            g_ref[hd * K + r] = p[r] / denom
        return carry

    lax.fori_loop(0, PEER_HEADS, head, 0)


def _peer_topk(scores):
    n_sets, n_blocks, n_keys, _ = scores.shape
    T = n_blocks * LANES
    E = PEER_HEADS * PEER_TOPK
    out_map = lambda i: (0, i, 0)
    e3, g3 = pl.pallas_call(
        _peer_topk_kernel,
        grid=(n_blocks // SUBLANES,),
        in_specs=[pl.BlockSpec((n_sets, SUBLANES, n_keys, LANES), lambda i: (0, i, 0, 0))],
        out_specs=[pl.BlockSpec((E, SUBLANES, LANES), out_map), pl.BlockSpec((E, SUBLANES, LANES), out_map)],
        out_shape=[jax.ShapeDtypeStruct((E, n_blocks, LANES), jnp.int32),
                   jax.ShapeDtypeStruct((E, n_blocks, LANES), F32)],
        scratch_shapes=[pltpu.VMEM((n_keys, SUBLANES, LANES), F32),
                        pltpu.VMEM((2, (2 * n_keys // PEER_TOPK - 1) * PEER_TOPK, SUBLANES, LANES), F32),
                        pltpu.VMEM((2, (2 * n_keys // PEER_TOPK - 1) * PEER_TOPK, SUBLANES, LANES), jnp.int32)],
        compiler_params=_params("parallel"),
        name="peer_topk",
    )(scores)
    return e3.reshape(E, T).T, g3.reshape(E, T).T


def _sum_over_sublanes(p, lo4, lo2, lo1):
    z = [jnp.where(lo4, p[k], p[k + 4]) + pltpu.roll(jnp.where(lo4, p[k + 4], p[k]), 4, axis=0) for k in range(4)]
    y = [jnp.where(lo2, z[k] + pltpu.roll(z[k], SUBLANES - 2, axis=0), z[k + 2] + pltpu.roll(z[k + 2], 2, axis=0))
         for k in range(2)]
    return jnp.where(lo1, y[0] + pltpu.roll(y[0], SUBLANES - 1, axis=0), y[1] + pltpu.roll(y[1], 1, axis=0))


def _peer_apply_kernel(ex_ref, exn_ref, g_ref, h2d_ref, ln_g_ref, ln_b_ref, uv_hbm, o_ref, buf, pbuf0, pbuf1, wbuf0, wbuf1,
                       h_ref, zbuf, sem):
    i = pl.program_id(0)
    n = pl.num_programs(0)
    tk, E = ex_ref.shape
    slot = lax.rem(i, 2)
    nslot = 1 - slot
    pbuf, wbuf = (pbuf0, pbuf1), (wbuf0, wbuf1)
    for c in range(SUBLANES):
        h_ref[:, c, :] = h2d_ref[:, c * LANES:(c + 1) * LANES]

    def row_copy(e, s, t, j):
        return pltpu.make_async_copy(uv_hbm.at[e], buf.at[s * tk + t, j], sem.at[s, t])

    def start_rows(idx_ref, s, t, j0, j1):
        for j in range(j0, j1):
            row_copy(idx_ref[t, j], s, t, j).start(priority=1 if (j % 16) % 2 == 1 and j % 16 != 15 else 0)

    def start_token(idx_ref, s, t):
        start_rows(idx_ref, s, t, 0, E)

    def wait_token(s, t):
        pltpu.make_async_copy(buf.at[s * tk + t], buf.at[s * tk + t], sem.at[s, t]).wait()

    @pl.when(i == 0)
    def _():
        def body(t, carry):
            start_token(ex_ref, slot, t)
            return carry
        lax.fori_loop(0, tk, body, 0)

    sub = lax.broadcasted_iota(jnp.int32, (SUBLANES, LANES), 0)
    lo4, lo2, lo1 = sub < 4, (sub & 2) == 0, (sub & 1) == 0
    eye = lax.broadcasted_iota(jnp.int32, (E, LANES), 0) == lax.broadcasted_iota(jnp.int32, (E, LANES), 1)
    n_acc = 4

    groups = E // SUBLANES
    per_chunk = E // groups

    def dots_group(t, g):
        h = h_ref[t]
        base = slot * tk + t
        return _sum_over_sublanes([buf[base, g * SUBLANES + k, 0:SUBLANES, :] * h for k in range(SUBLANES)],
                                  lo4, lo2, lo1)

    def weights(t, par):
        act = jnp.sum(pbuf[par][...], axis=1, keepdims=True)
        gate = jnp.sum(jnp.where(eye, g_ref[pl.ds(t, 1), :], 0.0), axis=1, keepdims=True)
        return jnp.broadcast_to(gate * _gelu(act), (E, LANES))

    def mix_group(t, par, g, acc):
        base = slot * tk + t
        for e in range(g * SUBLANES, (g + 1) * SUBLANES):
            w = jnp.broadcast_to(wbuf[par][e:e + 1, :], (SUBLANES, LANES))
            term = w * buf[base, e, SUBLANES:2 * SUBLANES, :]
            acc[e % n_acc] = term if acc[e % n_acc] is None else acc[e % n_acc] + term

    def stage(t, par, with_dots, with_weights, with_mix=True):
        if with_dots:
            wait_token(slot, t + 2)
        w_next = weights(t + 1, 1 - par) if with_weights else None
        acc, pending = [None] * n_acc, None
        for g in range(groups):
            if with_mix:
                start_rows(exn_ref, nslot, t, g * per_chunk, (g + 1) * per_chunk)
            if with_dots:
                p = dots_group(t + 2, g)
                if pending is not None:
                    pbuf[par][(g - 1) * SUBLANES:g * SUBLANES, :] = pending
                pending = p
            if with_mix:
                mix_group(t, par, g, acc)
        if with_dots:
            pbuf[par][(groups - 1) * SUBLANES:, :] = pending
        if with_mix:
            zbuf[t] = ALPHA * h_ref[t] + ((acc[0] + acc[1]) + (acc[2] + acc[3]))
        if with_weights:
            wbuf[1 - par][...] = w_next

    stage(-2, 0, True, False, with_mix=False)
    stage(-1, 1, True, True, with_mix=False)

    def token_pair(tt, carry):
        t = 2 * tt
        stage(t, 0, True, True)
        stage(t + 1, 1, True, True)
        return carry

    assert tk % 2 == 0 and tk >= 4
    lax.fori_loop(0, (tk - 2) // 2, token_pair, 0)

    stage(tk - 2, 0, False, True)
    stage(tk - 1, 1, False, False)


    z = zbuf[...]
    total = lambda a: jnp.sum(jnp.sum(a, axis=2, keepdims=True), axis=1, keepdims=True)
    inv_d = 1.0 / (SUBLANES * LANES)
    zc = z - total(z) * inv_d
    var = total(zc * zc) * inv_d
    zbuf[...] = zc * lax.rsqrt(var + EPS) * ln_g_ref[...] + ln_b_ref[...]
    for c in range(SUBLANES):
        o_ref[:, c * LANES:(c + 1) * LANES] = zbuf[:, c, :]

    @pl.when(i == n - 1)
    def _():
        def body(t, carry):
            wait_token(nslot, t)
            return carry
        lax.fori_loop(0, tk, body, 0)


def _peer_apply(h1, experts, gates, peer_u, peer_v, ln2_g, ln2_b, tk=16):
    T, D = h1.shape
    E = experts.shape[1]
    assert D == SUBLANES * LANES and E == LANES
    n = T // tk
    tile = lambda a: a.reshape(-1, SUBLANES, LANES)
    uv = jnp.concatenate([tile(peer_u), tile(peer_v)], axis=1)
    row = lambda i: (i, 0)
    nxt = lambda i: (jnp.minimum(i + 1, n - 1), 0)
    fixed = lambda i: (0, 0)
    return pl.pallas_call(
        _peer_apply_kernel,
        grid=(n,),
        in_specs=[pl.BlockSpec((tk, E), row, memory_space=pltpu.SMEM),
                  pl.BlockSpec((tk, E), nxt, memory_space=pltpu.SMEM),
                  pl.BlockSpec((tk, E), row), pl.BlockSpec((tk, D), row),
                  pl.BlockSpec((SUBLANES, LANES), fixed), pl.BlockSpec((SUBLANES, LANES), fixed),
                  pl.BlockSpec(memory_space=pl.ANY)],
        out_specs=pl.BlockSpec((tk, D), row),
        out_shape=jax.ShapeDtypeStruct((T, D), F32),
        scratch_shapes=[pltpu.VMEM((2 * tk, E, 2 * SUBLANES, LANES), F32)]
                       + [pltpu.VMEM((E, LANES), F32)] * 4
                       + [pltpu.VMEM((tk, SUBLANES, LANES), F32)] * 2 + [pltpu.SemaphoreType.DMA((2, tk))],
        compiler_params=_params("arbitrary"),
        name="peer_apply",
    )(experts, experts, gates, h1, ln2_g.reshape(SUBLANES, LANES), ln2_b.reshape(SUBLANES, LANES), uv)


def kernel(x, w_in, b_in, attn_sinks, conv_w, conv_b, gate_a_w, gate_a_b, gate_x_w, gate_x_b, lru_lambda, norm_attn_g, norm_rnn_g, w_out, b_out, ln1_g, ln1_b, peer_w_q, peer_keys_1, peer_keys_2, peer_u, peer_v, ln2_g, ln2_b):
    B, S, D = x.shape
    x2 = x.reshape(B * S, D)
    rnn_width = conv_w.shape[1]
    q, k, v, xr, gate_in = _in_proj(x2, w_in, b_in, rnn_width)
    mixed_attn = _attention(q, k, v, attn_sinks, norm_attn_g, B, S).reshape(B * S, ATTN_WIDTH)
    mixed_rnn = _rglru(xr, gate_in, conv_w, conv_b, gate_a_w, gate_a_b, gate_x_w, gate_x_b, lru_lambda,
                       norm_rnn_g, B, S).reshape(B * S, rnn_width)
    h1, scores_t = _post_mix(x2, mixed_attn, mixed_rnn, w_out, b_out, ln1_g, ln1_b, peer_w_q,
                             peer_keys_1, peer_keys_2)
    experts, gates = _peer_topk(scores_t)
    out = _peer_apply(h1, experts, gates, peer_u, peer_v, ln2_g, ln2_b)
    return out.reshape(B, S, D)
```
